```python
import math
import jax, jax.numpy as jnp
from jax import lax
import numpy as np

D_MODEL = 1024
BATCH = 2
SEQ = 16384
DEPTH = 2

GRID_W = 64
CTX_LEN = 256
EPS = 1e-6
F32 = jnp.float32

N_EVEN = (DEPTH + 1) // 2
N_ODD = DEPTH // 2

DA_HEADS = 4
DA_QK_DIM = 64
DA_V_DIM = 2 * DA_QK_DIM
DA_SCALE = DA_QK_DIM ** -0.5
DA_QBLOCK = 128
ROPE_THETA = 10000.0
ROPE_PAIRS = DA_QK_DIM // 4
DA_QW = DA_HEADS * 2 * DA_QK_DIM
DA_VW = DA_HEADS * DA_V_DIM
DA_IN = 2 * DA_QW + DA_VW

GDN_HEADS = 4
GDN_DK = 128
GDN_DV = 128
GDN_CONV = 4
GDN_CHUNK = 64
GDN_W = GDN_HEADS * GDN_DK
GDN_VW = GDN_HEADS * GDN_DV
GDN_IN = 2 * GDN_W + 2 * GDN_VW + 4 * GDN_HEADS

L0_IN = DA_IN + GDN_IN
MIX0_OUT = DA_VW + GDN_VW

LRU_WIDTH = D_MODEL
LRU_BLOCKS = 8
LRU_BW = LRU_WIDTH // LRU_BLOCKS
LRU_CONV = 4
LRU_C = 8.0

FFN_HIDDEN = int(math.ceil(8 * D_MODEL / 3 / 256)) * 256

kernel_name = "hybrid_diffattn_gdn_rglru_prefix_dit"


def rmsnorm(x, gain):
    xf = x.astype(F32)
    y = xf * lax.rsqrt(jnp.mean(xf * xf, axis=-1, keepdims=True) + EPS)
    return (y * gain.astype(F32)).astype(x.dtype)


def l2norm(x):
    xf = x.astype(F32)
    return (xf * lax.rsqrt(jnp.sum(xf * xf, axis=-1, keepdims=True) + EPS)).astype(x.dtype)


def ada_mod(cond, w, b):
    m = jax.nn.silu(cond) @ w + b
    return jnp.split(m[..., None, :], 6, axis=-1)


def modulate(h, gain, shift, scale):
    return rmsnorm(h, gain) * (1.0 + scale) + shift


def swiglu(h, w_gu, w_down):
    gu = h @ w_gu
    return (jax.nn.silu(gu[..., :FFN_HIDDEN]) * gu[..., FFN_HIDDEN:]) @ w_down


def short_conv_centred(x, w):
    K = w.shape[0]
    n = x.shape[1]
    left = K // 2
    xp = jnp.pad(x, ((0, 0), (left, K - 1 - left), (0, 0)))
    out = xp[:, 0:n] * w[0]
    for j in range(1, K):
        out = out + xp[:, j:j + n] * w[j]
    return out


def axial_rope_tables(row, col):
    inv = ROPE_THETA ** (-jnp.arange(ROPE_PAIRS, dtype=F32) / ROPE_PAIRS)
    ang = jnp.stack([row.astype(F32)[:, None] * inv, col.astype(F32)[:, None] * inv], axis=1)
    return jnp.cos(ang), jnp.sin(ang)


def apply_axial_rope(x, cos, sin):
    shp = x.shape
    xr = x.astype(F32).reshape(shp[:-1] + (2, 2, ROPE_PAIRS))
    x1, x2 = xr[..., 0, :], xr[..., 1, :]
    c = cos[:, None, None]
    s = sin[:, None, None]
    out = jnp.stack([x1 * c - x2 * s, x2 * c + x1 * s], axis=-2)
    return out.reshape(shp).astype(x.dtype)


def diff_softmax_attend(q, k, v, lam):
    s = jnp.einsum('bqhmd,bkhmd->bhmqk', q, k, preferred_element_type=F32) * DA_SCALE
    p = jax.nn.softmax(s, axis=-1)
    w = p[:, :, 0] - lam * p[:, :, 1]
    return jnp.einsum('bhqk,bkhe->bqhe', w.astype(v.dtype), v)


def diff_attn_blocked(q, k, v, lam):
    B, T = q.shape[:2]
    nb = T // DA_QBLOCK
    qb = q.reshape((B, nb, DA_QBLOCK) + q.shape[2:]).swapaxes(0, 1)
    ob = lax.map(lambda qq: diff_softmax_attend(qq, k, v, lam), qb)
    return ob.swapaxes(0, 1).reshape((B, T) + ob.shape[3:])


def diff_attention(u_c, u_x, cos, sin, lambda_init, need_ctx,
                   q_norm, k_norm, lam_q1, lam_k1, lam_q2, lam_k2, sub_norm):
    def split(u):
        B, n, _ = u.shape
        q = rmsnorm(u[..., :DA_QW].reshape(B, n, DA_HEADS, 2, DA_QK_DIM), q_norm)
        k = rmsnorm(u[..., DA_QW:2 * DA_QW].reshape(B, n, DA_HEADS, 2, DA_QK_DIM), k_norm)
        v = u[..., 2 * DA_QW:DA_IN].reshape(B, n, DA_HEADS, DA_V_DIM)
        return q, k, v

    qc, kc, vc = split(u_c)
    qx, kx, vx = split(u_x)
    qx = apply_axial_rope(qx, cos, sin)
    kx = apply_axial_rope(kx, cos, sin)
    lam = (jnp.exp(jnp.sum(lam_q1.astype(F32) * lam_k1.astype(F32)))
           - jnp.exp(jnp.sum(lam_q2.astype(F32) * lam_k2.astype(F32))) + lambda_init)

    def finish(o):
        o = rmsnorm(o, sub_norm) * (1.0 - lambda_init)
        return o.reshape(o.shape[:2] + (-1,))

    k_all = jnp.concatenate([kc, kx], axis=1)
    v_all = jnp.concatenate([vc, vx], axis=1)
    ox = finish(diff_attn_blocked(qx, k_all, v_all, lam))
    oc = finish(diff_softmax_attend(qc, kc, vc, lam)) if need_ctx else None
    return oc, ox


def gated_delta_chunked(q, k, v, g, beta, state0):
    B, T, H, dk = q.shape
    dv = v.shape[-1]
    N = T // GDN_CHUNK

    def chunks(a):
        a = a.astype(F32).reshape((B, N, GDN_CHUNK) + a.shape[2:])
        return jnp.moveaxis(jnp.moveaxis(a, 1, 0), 3, 2)

    q = chunks(q) * dk ** -0.5
    k, v, g, beta = chunks(k), chunks(v), chunks(g), chunks(beta)
    gcum = jnp.cumsum(g, axis=-1)
    idx = jnp.arange(GDN_CHUNK)
    lower = idx[:, None] >= idx[None, :]
    strict = idx[:, None] > idx[None, :]
    decay = jnp.exp(jnp.where(lower, gcum[..., :, None] - gcum[..., None, :], -jnp.inf))
    kb = k * beta[..., None]
    a_in = jnp.where(strict, jnp.einsum('nbhid,nbhjd->nbhij', kb, k) * decay, 0.0)
    a_in = a_in + jnp.eye(GDN_CHUNK, dtype=F32)
    rhs = jnp.concatenate([v * beta[..., None], kb * jnp.exp(gcum)[..., None]], axis=-1)
    sol = lax.linalg.triangular_solve(a_in, rhs, left_side=True, lower=True)
    u, w = sol[..., :dv], sol[..., dv:]
    attn = jnp.einsum('nbhid,nbhjd->nbhij', q, k) * decay

    def step(S, xs):
        q_i, k_i, u_i, w_i, g_i, attn_i = xs
        v_new = u_i - w_i @ S
        o = (q_i * jnp.exp(g_i)[..., None]) @ S + attn_i @ v_new
        g_last = g_i[..., -1:]
        S = S * jnp.exp(g_last)[..., None] + jnp.einsum(
            'bhcd,bhce->bhde', k_i * jnp.exp(g_last - g_i)[..., None], v_new)
        return S, o

    S, o = lax.scan(step, state0.astype(F32), (q, k, u, w, gcum, attn))
    o = jnp.moveaxis(jnp.moveaxis(o, 2, 3), 0, 1).reshape(B, T, H, dv)
    return S, o


def gated_deltanet(u_c, u_x, conv_w, a_log, dt_bias, o_norm):
    def prep(u):
        B, n, _ = u.shape
        qkv = jax.nn.silu(short_conv_centred(u[..., :2 * GDN_W + GDN_VW], conv_w))
        q = l2norm(qkv[..., :GDN_W].reshape(B, n, GDN_HEADS, GDN_DK))
        k = l2norm(qkv[..., GDN_W:2 * GDN_W].reshape(B, n, GDN_HEADS, GDN_DK))
        v = qkv[..., 2 * GDN_W:].reshape(B, n, GDN_HEADS, GDN_DV)
        z = u[..., 2 * GDN_W + GDN_VW:2 * GDN_W + 2 * GDN_VW].reshape(B, n, GDN_HEADS, GDN_DV)
        s = u[..., 2 * GDN_W + 2 * GDN_VW:].astype(F32).reshape(B, n, 2, 2, GDN_HEADS)
        beta = jax.nn.sigmoid(s[:, :, 0])
        g = -jnp.exp(a_log.astype(F32)) * jax.nn.softplus(s[:, :, 1] + dt_bias.astype(F32))
        return q, k, v, z, beta, g

    qc, kc, vc, zc, bc, g_c = prep(u_c)
    qx, kx, vx, zx, bx, g_x = prep(u_x)
    fl = lambda a: jnp.flip(a, axis=1)
    s0 = jnp.zeros((u_x.shape[0], GDN_HEADS, GDN_DK, GDN_DV), F32)
    sc_f, oc_f = gated_delta_chunked(qc, kc, vc, g_c[:, :, 0], bc[:, :, 0], s0)
    _, ox_f = gated_delta_chunked(qx, kx, vx, g_x[:, :, 0], bx[:, :, 0], sc_f)
    sc_b, oc_b = gated_delta_chunked(fl(qc), fl(kc), fl(vc), fl(g_c[:, :, 1]), fl(bc[:, :, 1]), s0)
    _, ox_b = gated_delta_chunked(fl(qx), fl(kx), fl(vx), fl(g_x[:, :, 1]), fl(bx[:, :, 1]), sc_b)

    def out(o, z, dt):
        o = rmsnorm(o, o_norm) * jax.nn.silu(z.astype(F32))
        return o.reshape(o.shape[:2] + (-1,)).astype(dt)

    return out(oc_f + fl(oc_b), zc, u_c.dtype), out(ox_f + fl(ox_b), zx, u_x.dtype)


def rglru_gates(xr, conv_w, conv_b, w_r, b_r, w_i, b_i, lam):
    xc = (short_conv_centred(xr, conv_w) + conv_b).astype(F32)
    B, n, W = xc.shape
    xb = xc.reshape(B, n, LRU_BLOCKS, LRU_BW)
    r = jax.nn.sigmoid(jnp.einsum('bnkc,zkcd->zbnkd', xb, w_r.astype(F32)).reshape(2, B, n, W)
                       + b_r.astype(F32)[:, None, None])
    i = jax.nn.sigmoid(jnp.einsum('bnkc,zkcd->zbnkd', xb, w_i.astype(F32)).reshape(2, B, n, W)
                       + b_i.astype(F32)[:, None, None])
    log_a = -LRU_C * r * jax.nn.softplus(-lam.astype(F32))[:, None, None]
    a = jnp.exp(log_a)
    b = jnp.sqrt(-jnp.expm1(2.0 * log_a)) * (i * xc[None])
    return a, b


def linear_recurrence(a, b, h0):
    b = b.at[:, 0].add(a[:, 0] * h0)
    _, h = lax.associative_scan(lambda l, r: (l[0] * r[0], r[0] * l[1] + r[1]), (a, b), axis=1)
    return h


def bidir_rglru(a_c, b_c, a_x, b_x):
    fl = lambda t: jnp.flip(t, axis=1)
    h0 = jnp.zeros_like(b_x[0, :, 0])
    hc_f = linear_recurrence(a_c[0], b_c[0], h0)
    hx_f = linear_recurrence(a_x[0], b_x[0], hc_f[:, -1])
    hc_b = linear_recurrence(fl(a_c[1]), fl(b_c[1]), h0)
    hx_b = linear_recurrence(fl(a_x[1]), fl(b_x[1]), hc_b[:, -1])
    return hc_f + fl(hc_b), hx_f + fl(hx_b)


def even_layer(hc, hx, c, c_ctx, cos, sin, lambda_init, last,
               norm1, norm2, ada_w, ada_b, w_in, w_out,
               q_norm, k_norm, lam_q1, lam_k1, lam_q2, lam_k2, sub_norm,
               gdn_conv, gdn_a_log, gdn_dt_bias, gdn_o_norm, w_gu, w_down):
    mx = ada_mod(c, ada_w, ada_b)
    mc = ada_mod(c_ctx, ada_w, ada_b)
    uc = modulate(hc, norm1, mc[0], mc[1]) @ w_in
    ux = modulate(hx, norm1, mx[0], mx[1]) @ w_in
    ac, ax = diff_attention(uc[..., :DA_IN], ux[..., :DA_IN], cos, sin, lambda_init, not last,
                            q_norm, k_norm, lam_q1, lam_k1, lam_q2, lam_k2, sub_norm)
    bc, bx = gated_deltanet(uc[..., DA_IN:], ux[..., DA_IN:], gdn_conv, gdn_a_log, gdn_dt_bias, gdn_o_norm)
    hx = hx + mx[2] * (jnp.concatenate([ax, bx], axis=-1) @ w_out)
    hx = hx + mx[5] * swiglu(modulate(hx, norm2, mx[3], mx[4]), w_gu, w_down)
    if not last:
        hc = hc + mc[2] * (jnp.concatenate([ac, bc], axis=-1) @ w_out)
        hc = hc + mc[5] * swiglu(modulate(hc, norm2, mc[3], mc[4]), w_gu, w_down)
    return hc, hx


def odd_layer(hc, hx, c, c_ctx, last, norm1, norm2, ada_w, ada_b, w_in, conv_w, conv_b,
              w_r, b_r, w_i, b_i, lam, w_out, w_gu, w_down):
    mx = ada_mod(c, ada_w, ada_b)
    mc = ada_mod(c_ctx, ada_w, ada_b)
    ux = modulate(hx, norm1, mx[0], mx[1]) @ w_in
    w_c = w_in[:, LRU_WIDTH:] if last else w_in
    uc = modulate(hc, norm1, mc[0], mc[1]) @ w_c
    a_c, b_c = rglru_gates(uc[..., -LRU_WIDTH:], conv_w, conv_b, w_r, b_r, w_i, b_i, lam)
    a_x, b_x = rglru_gates(ux[..., LRU_WIDTH:], conv_w, conv_b, w_r, b_r, w_i, b_i, lam)
    h_c, h_x = bidir_rglru(a_c, b_c, a_x, b_x)

    def out(u, h):
        return (jax.nn.gelu(u[..., :LRU_WIDTH]) * h.astype(u.dtype)) @ w_out

    hx = hx + mx[2] * out(ux, h_x)
    hx = hx + mx[5] * swiglu(modulate(hx, norm2, mx[3], mx[4]), w_gu, w_down)
    if not last:
        hc = hc + mc[2] * out(uc, h_c)
        hc = hc + mc[5] * swiglu(modulate(hc, norm2, mc[3], mc[4]), w_gu, w_down)
    return hc, hx


def setup_inputs(seed: int = 0) -> dict:
    key = jax.random.key(seed)
    keys = iter(jax.random.split(key, 64))

    def normal(shape, scale):
        return scale * jax.random.normal(next(keys), shape, F32)

    def gain(shape):
        return 1.0 + 0.05 * jax.random.normal(next(keys), shape, F32)

    def uniform(shape, lo, hi):
        return jax.random.uniform(next(keys), shape, F32, lo, hi)

    D, NE, NO = D_MODEL, N_EVEN, N_ODD
    dt = jnp.exp(uniform((NE, 2, GDN_HEADS), math.log(1e-3), math.log(1e-1)))
    sg = uniform((NO, 2, LRU_WIDTH), 0.9, 0.999) ** (1.0 / LRU_C)
    return {
        "x": normal((BATCH, SEQ, D), 1.0),
        "c": normal((BATCH, D), 1.0),
        "ctx": normal((BATCH, CTX_LEN, D), 1.0),
        "c_ctx": normal((D,), 1.0),
        "ev_norm1": gain((NE, D)),
        "ev_norm2": gain((NE, D)),
        "ev_ada_w": normal((NE, D, 6 * D), 0.5 * D ** -0.5),
        "ev_ada_b": normal((NE, 6 * D), 0.02),
        "ev_w_in": normal((NE, D, L0_IN), D ** -0.5),
        "ev_w_out": normal((NE, MIX0_OUT, D), MIX0_OUT ** -0.5),
        "ev_q_norm": gain((NE, DA_QK_DIM)),
        "ev_k_norm": gain((NE, DA_QK_DIM)),
        "ev_lam_q1": normal((NE, DA_QK_DIM), 0.1),
        "ev_lam_k1": normal((NE, DA_QK_DIM), 0.1),
        "ev_lam_q2": normal((NE, DA_QK_DIM), 0.1),
        "ev_lam_k2": normal((NE, DA_QK_DIM), 0.1),
        "ev_sub_norm": gain((NE, DA_V_DIM)),
        "ev_gdn_conv": normal((NE, GDN_CONV, 2 * GDN_W + GDN_VW), GDN_CONV ** -0.5),
        "ev_gdn_a_log": jnp.log(uniform((NE, 2, GDN_HEADS), 1.0, 16.0)),
        "ev_gdn_dt_bias": dt + jnp.log(-jnp.expm1(-dt)),
        "ev_gdn_o_norm": gain((NE, GDN_DV)),
        "ev_ffn_w_gu": normal((NE, D, 2 * FFN_HIDDEN), D ** -0.5),
        "ev_ffn_w_down": normal((NE, FFN_HIDDEN, D), FFN_HIDDEN ** -0.5),
        "od_norm1": gain((NO, D)),
        "od_norm2": gain((NO, D)),
        "od_ada_w": normal((NO, D, 6 * D), 0.5 * D ** -0.5),
        "od_ada_b": normal((NO, 6 * D), 0.02),
        "od_w_in": normal((NO, D, 2 * LRU_WIDTH), D ** -0.5),
        "od_conv_w": normal((NO, LRU_CONV, LRU_WIDTH), LRU_CONV ** -0.5),
        "od_conv_b": normal((NO, LRU_WIDTH), 0.02),
        "od_w_r": normal((NO, 2, LRU_BLOCKS, LRU_BW, LRU_BW), LRU_BW ** -0.5),
        "od_b_r": normal((NO, 2, LRU_WIDTH), 0.02),
        "od_w_i": normal((NO, 2, LRU_BLOCKS, LRU_BW, LRU_BW), LRU_BW ** -0.5),
        "od_b_i": normal((NO, 2, LRU_WIDTH), 0.02),
        "od_lam": jnp.log(sg) - jnp.log1p(-sg),
        "od_w_out": normal((NO, LRU_WIDTH, D), LRU_WIDTH ** -0.5),
        "od_ffn_w_gu": normal((NO, D, 2 * FFN_HIDDEN), D ** -0.5),
        "od_ffn_w_down": normal((NO, FFN_HIDDEN, D), FFN_HIDDEN ** -0.5),
    }


def reference(x, c, ctx, c_ctx,
              ev_norm1, ev_norm2, ev_ada_w, ev_ada_b, ev_w_in, ev_w_out,
              ev_q_norm, ev_k_norm, ev_lam_q1, ev_lam_k1, ev_lam_q2, ev_lam_k2, ev_sub_norm,
              ev_gdn_conv, ev_gdn_a_log, ev_gdn_dt_bias, ev_gdn_o_norm,
              ev_ffn_w_gu, ev_ffn_w_down,
              od_norm1, od_norm2, od_ada_w, od_ada_b, od_w_in, od_conv_w, od_conv_b,
              od_w_r, od_b_r, od_w_i, od_b_i, od_lam, od_w_out,
              od_ffn_w_gu, od_ffn_w_down):
    n = x.shape[1]
    rows = n // GRID_W
    row = jnp.repeat(jnp.arange(rows), GRID_W)
    col = jnp.tile(jnp.arange(GRID_W), rows)
    cos, sin = axial_rope_tables(row, col)
    hc, hx = ctx, x
    for layer in range(DEPTH):
        last = layer == DEPTH - 1
        j = layer // 2
        if layer % 2 == 0:
            lambda_init = 0.8 - 0.6 * math.exp(-0.3 * layer)
            hc, hx = even_layer(hc, hx, c, c_ctx, cos, sin, lambda_init, last,
                                ev_norm1[j], ev_norm2[j], ev_ada_w[j], ev_ada_b[j], ev_w_in[j], ev_w_out[j],
                                ev_q_norm[j], ev_k_norm[j], ev_lam_q1[j], ev_lam_k1[j], ev_lam_q2[j],
                                ev_lam_k2[j], ev_sub_norm[j], ev_gdn_conv[j], ev_gdn_a_log[j],
                                ev_gdn_dt_bias[j], ev_gdn_o_norm[j], ev_ffn_w_gu[j], ev_ffn_w_down[j])
        else:
            hc, hx = odd_layer(hc, hx, c, c_ctx, last,
                               od_norm1[j], od_norm2[j], od_ada_w[j], od_ada_b[j], od_w_in[j],
                               od_conv_w[j], od_conv_b[j], od_w_r[j], od_b_r[j], od_w_i[j], od_b_i[j],
                               od_lam[j], od_w_out[j], od_ffn_w_gu[j], od_ffn_w_down[j])
    return hx
```

```python
import functools
import math

import jax
import jax.numpy as jnp
from jax import lax
from jax.experimental import pallas as pl
from jax.experimental.pallas import tpu as pltpu

F32 = jnp.float32
BF16 = jnp.bfloat16
EPS = 1e-6
HIGHEST = lax.Precision.HIGHEST

GRID_W = 64
ROPE_THETA = 10000.0
DA_HEADS = 4
DA_QK = 64
DA_V = 128
DA_SCALE = DA_QK ** -0.5
ROPE_PAIRS = DA_QK // 4
GDN_HEADS = 4
GDN_DK = 128
GDN_CHUNK = 64
GDN_STACK = GDN_HEADS * GDN_CHUNK
LRU_BLOCKS = 8
LRU_C = 8.0
LOG2E = math.log2(math.e)

ROW_TILE = 256
VMEM_LIMIT = 56 * 1024 * 1024


def _dot(a, b):
    return jnp.dot(a.astype(BF16), b.astype(BF16), preferred_element_type=F32)


def _dot_nt(a, b):
    return lax.dot_general(a.astype(BF16), b.astype(BF16), (((1,), (1,)), ((), ())),
                           preferred_element_type=F32)


def _dot_hi(a, b):
    return jnp.dot(a, b, precision=HIGHEST, preferred_element_type=F32)


def _silu(x):
    return x * jax.nn.sigmoid(x)


def _modulate(x, gain, shift, scale):
    ms = jnp.mean(x * x, axis=-1, keepdims=True)
    return x * lax.rsqrt(ms + EPS) * gain * (1.0 + scale) + shift


def _params(*sem):
    return pltpu.CompilerParams(dimension_semantics=sem, vmem_limit_bytes=VMEM_LIMIT)


def _ada_kernel(c_ref, w_ref, b_ref, o_ref):
    o_ref[...] = _dot_hi(_silu(c_ref[...]), w_ref[...]) + b_ref[...]


def _ada_mod(cond, w, b):
    d = w.shape[0]
    n = w.shape[1]
    return pl.pallas_call(
        _ada_kernel,
        grid=(n // d,),
        in_specs=[pl.BlockSpec((8, d), lambda j: (0, 0)),
                  pl.BlockSpec((d, d), lambda j: (0, j)),
                  pl.BlockSpec((1, d), lambda j: (0, j))],
        out_specs=pl.BlockSpec((8, d), lambda j: (0, j)),
        out_shape=jax.ShapeDtypeStruct((8, n), F32),
        compiler_params=_params("parallel"),
        name="ada_mod",
    )(cond, w, b.reshape(1, n))


def _mod_table(m, batch, d):
    six = m.reshape(8, 6, d)
    lat = six[:batch]
    ctx = jnp.broadcast_to(six[batch][None], (batch, 6, d))
    t = jnp.stack([lat, ctx], axis=1)
    return jnp.pad(t, ((0, 0), (0, 0), (0, 2), (0, 0)))


def _rope(x, cos, sin_signed):
    n = x.shape[-1]
    lane = lax.broadcasted_iota(jnp.int32, x.shape, 1) % (2 * ROPE_PAIRS)
    partner = jnp.where(lane < ROPE_PAIRS, pltpu.roll(x, n - ROPE_PAIRS, 1), pltpu.roll(x, ROPE_PAIRS, 1))
    return x * cos + partner * sin_signed


def _inproj0_kernel(h_ref, mod_ref, n1_ref, wq_ref, wk_ref, wv_ref, wg_ref, wz_ref, wgt_ref, gsum_ref,
                    qg_ref, kg_ref, cos_ref, sin_ref,
                    qz_ref, k_ref, vt_ref, ug_ref, z_ref, gt_ref, *, n_lat_tiles):
    is_lat = pl.program_id(1) < n_lat_tiles
    y = _modulate(h_ref[0], n1_ref[...], mod_ref[0, 0, 0:1, :], mod_ref[0, 0, 1:2, :])
    yb = y.astype(BF16)
    cos = jnp.concatenate([cos_ref[...]] * DA_HEADS, axis=1)
    sin = jnp.concatenate([sin_ref[...]] * DA_HEADS, axis=1)

    def head_norm_rope(w_ref, gain_ref):
        u = _dot(yb, w_ref[...])
        sq = u * u
        hi = sq.astype(BF16)
        lo = (sq - hi.astype(F32)).astype(BF16)
        ss = _dot(hi, gsum_ref[...]) + _dot(lo, gsum_ref[...])
        un = u * lax.rsqrt(ss * (1.0 / DA_QK) + EPS) * gain_ref[...]
        return jnp.where(is_lat, _rope(un, cos, sin), un)

    q = head_norm_rope(wq_ref, qg_ref) * (DA_SCALE * LOG2E)
    k = head_norm_rope(wk_ref, kg_ref)
    low = lax.broadcasted_iota(jnp.int32, (q.shape[0], DA_V), 1) < DA_QK
    parts = []
    for hd in range(DA_HEADS):
        qh = q[:, hd * DA_V:(hd + 1) * DA_V]
        parts += [jnp.where(low, qh, 0.0), jnp.where(low, 0.0, qh)]
    qz_ref[0] = jnp.concatenate(parts, axis=1).astype(BF16)
    k_ref[0] = k.astype(BF16)
    vt_ref[0] = _dot(yb, wv_ref[...]).T.astype(BF16)
    ug_ref[0] = _dot(yb, wg_ref[...])
    z_ref[0] = _dot(yb, wz_ref[...])
    gt_ref[0] = _dot(yb, wgt_ref[...])


def _inproj0(h, mod, norm1, wq, wk, wv, wg, wz, wgt, gsum, qgain, kgain, cos, sin, s_len):
    b, r, d = h.shape
    tm = ROW_TILE
    n_lat = s_len // tm
    full = lambda a: pl.BlockSpec(a.shape, lambda bi, i: (0,) * a.ndim)
    rows = lambda w: pl.BlockSpec((1, tm, w), lambda bi, i: (bi, i, 0))
    tab = pl.BlockSpec((tm, 128), lambda bi, i: (jnp.minimum(i, n_lat - 1), 0))
    return pl.pallas_call(
        functools.partial(_inproj0_kernel, n_lat_tiles=n_lat),
        grid=(b, r // tm),
        in_specs=[rows(d),
                  pl.BlockSpec((1, 1, 8, d), lambda bi, i: (bi, jnp.where(i < n_lat, 0, 1), 0, 0)),
                  full(norm1), full(wq), full(wk), full(wv), full(wg), full(wz), full(wgt), full(gsum),
                  full(qgain), full(kgain), tab, tab],
        out_specs=[rows(2 * DA_HEADS * DA_V), rows(DA_HEADS * DA_V),
                   pl.BlockSpec((1, DA_HEADS * DA_V, tm), lambda bi, i: (bi, 0, i)),
                   rows(wg.shape[1]), rows(wz.shape[1]), rows(128)],
        out_shape=[jax.ShapeDtypeStruct((b, r, 2 * DA_HEADS * DA_V), BF16),
                   jax.ShapeDtypeStruct((b, r, DA_HEADS * DA_V), BF16),
                   jax.ShapeDtypeStruct((b, DA_HEADS * DA_V, r), BF16),
                   jax.ShapeDtypeStruct((b, r, wg.shape[1]), F32),
                   jax.ShapeDtypeStruct((b, r, wz.shape[1]), F32),
                   jax.ShapeDtypeStruct((b, r, 128), F32)],
        compiler_params=_params("parallel", "parallel"),
        name="inproj0",
    )(h, mod, norm1, wq, wk, wv, wg, wz, wgt, gsum, qgain, kgain, cos, sin)


def _attn_kernel(qz_ref, k_ref, vt_ref, sub_ref, lam_ref, o_ref, acc_ref, m_ref, l_ref,
                 *, tk, n_lat_tiles, n_kv_lat, n_kv, lambda_init):
    is_ctx = pl.program_id(2) >= n_lat_tiles
    m_ref[...] = jnp.full(m_ref.shape, -1e30, F32)
    l_ref[...] = jnp.zeros(l_ref.shape, F32)
    acc_ref[...] = jnp.zeros(acc_ref.shape, F32)
    qs = (qz_ref[0, :, 0:DA_V], qz_ref[0, :, DA_V:2 * DA_V])

    def body(j, carry):
        off = pl.multiple_of(j * tk, tk)
        kc = k_ref[0, pl.ds(off, tk), :]
        vc = vt_ref[0, :, pl.ds(off, tk)]
        for mi in range(2):
            st = lax.dot_general(kc, qs[mi], (((1,), (1,)), ((), ())), preferred_element_type=F32)
            m_old = m_ref[mi]
            m_new = jnp.maximum(m_old, jnp.max(st, axis=0, keepdims=True))
            alpha = jnp.exp2(m_old - m_new)
            p = jnp.exp2(st - m_new)
            l_ref[mi] = l_ref[mi] * alpha + jnp.sum(p, axis=0, keepdims=True)
            acc_ref[mi] = acc_ref[mi] * alpha + jnp.dot(vc, p.astype(BF16), preferred_element_type=F32)
            m_ref[mi] = m_new
        return carry

    lax.fori_loop(jnp.where(is_ctx, n_kv_lat, 0), n_kv, body, 0)

    lam_rows = lam_ref[...]
    e1 = jnp.exp(jnp.sum(lam_rows[0:1] * lam_rows[1:2], axis=-1, keepdims=True))
    e2 = jnp.exp(jnp.sum(lam_rows[2:3] * lam_rows[3:4], axis=-1, keepdims=True))
    lam = e1 - e2 + lambda_init
    o = acc_ref[0] / l_ref[0] - lam * (acc_ref[1] / l_ref[1])
    ms = jnp.mean(o * o, axis=0, keepdims=True)
    y = o * lax.rsqrt(ms + EPS) * sub_ref[...] * (1.0 - lambda_init)
    o_ref[0] = y.T.astype(BF16)


def _diff_attention(qz, k, vt, sub_col, lam_rows, s_len, lambda_init):
    b, r, _ = k.shape
    tq = ROW_TILE
    tk = ROW_TILE
    kern = functools.partial(_attn_kernel, tk=tk, n_lat_tiles=s_len // tq, n_kv_lat=s_len // tk,
                             n_kv=r // tk, lambda_init=lambda_init)
    return pl.pallas_call(
        kern,
        grid=(b, DA_HEADS, r // tq),
        in_specs=[pl.BlockSpec((1, tq, 2 * DA_V), lambda bi, hd, i: (bi, i, hd)),
                  pl.BlockSpec((1, r, DA_V), lambda bi, hd, i: (bi, 0, hd)),
                  pl.BlockSpec((1, DA_V, r), lambda bi, hd, i: (bi, hd, 0)),
                  pl.BlockSpec((DA_V, 1), lambda bi, hd, i: (0, 0)),
                  pl.BlockSpec((4, 128), lambda bi, hd, i: (0, 0))],
        out_specs=pl.BlockSpec((1, tq, DA_V), lambda bi, hd, i: (bi, i, hd)),
        out_shape=jax.ShapeDtypeStruct((b, r, DA_HEADS * DA_V), BF16),
        scratch_shapes=[pltpu.VMEM((2, DA_V, tq), F32), pltpu.VMEM((2, 1, tq), F32),
                        pltpu.VMEM((2, 1, tq), F32)],
        compiler_params=_params("parallel", "parallel", "arbitrary"),
        name="diff_attn",
    )(qz, k, vt, sub_col, lam_rows)


def _conv_taps(ext_ref, prev_ref, x_ref, next_ref, w_ref, tile, s_len, r_len):
    tm = x_ref.shape[1]
    r0 = tile * tm
    has_prev = jnp.logical_and(r0 != 0, r0 != s_len)
    has_next = jnp.logical_and(r0 + tm != s_len, r0 + tm != r_len)
    x = x_ref[0]
    ext_ref[0:8, :] = jnp.where(has_prev, prev_ref[0], 0.0)
    ext_ref[8:8 + tm, :] = x
    ext_ref[8 + tm:16 + tm, :] = jnp.where(has_next, next_ref[0], 0.0)
    w = w_ref[...]
    return (ext_ref[pl.ds(6, tm), :] * w[0:1] + ext_ref[pl.ds(7, tm), :] * w[1:2]
            + x * w[2:3] + ext_ref[pl.ds(9, tm), :] * w[3:4])


def _halo_specs(tm, width, tile_of, n_rows):
    blocks = n_rows // 8
    cur = lambda *g: tile_of(*g)
    return [pl.BlockSpec((1, 8, width), lambda *g: (g[0], jnp.maximum(cur(*g) * (tm // 8) - 1, 0), 0)),
            pl.BlockSpec((1, tm, width), lambda *g: (g[0], cur(*g), 0)),
            pl.BlockSpec((1, 8, width), lambda *g: (g[0], jnp.minimum((cur(*g) + 1) * (tm // 8), blocks - 1), 0))]


def _col(x, lane):
    return x[:, lane:lane + 1]


def _gdn_prep_kernel(prev_ref, x_ref, next_ref, gt_ref, cw_ref, arow_ref, dtrow_ref,
                     w_ref, u_ref, qg_ref, kdt_ref, att_ref, gl_ref, ext_ref, *, s_len, r_len):
    tile = pl.program_id(1)
    tm = x_ref.shape[1]
    hw = GDN_HEADS * GDN_DK
    qkv = _silu(_conv_taps(ext_ref, prev_ref, x_ref, next_ref, cw_ref, tile, s_len, r_len))

    def l2n(a):
        return a * lax.rsqrt(jnp.sum(a * a, axis=-1, keepdims=True) + EPS)

    qh = [l2n(qkv[:, hd * GDN_DK:(hd + 1) * GDN_DK]) * GDN_DK ** -0.5 for hd in range(GDN_HEADS)]
    kh = [l2n(qkv[:, hw + hd * GDN_DK:hw + (hd + 1) * GDN_DK]) for hd in range(GDN_HEADS)]
    vh = [qkv[:, 2 * hw + hd * GDN_DK:2 * hw + (hd + 1) * GDN_DK] for hd in range(GDN_HEADS)]

    gt = gt_ref[0]
    beta_all = jax.nn.sigmoid(gt)
    g_all = -jnp.exp(arow_ref[...]) * jax.nn.softplus(gt + dtrow_ref[...])

    c = GDN_CHUNK
    st = GDN_STACK
    rr = lax.broadcasted_iota(jnp.int32, (st, st), 0)
    cc = lax.broadcasted_iota(jnp.int32, (st, st), 1)
    same = (rr // c) == (cc // c)
    eye = rr == cc
    row_in_chunk = lax.broadcasted_iota(jnp.int32, (c, 128), 0)

    for ci in range(tm // c):
        sl = slice(ci * c, (ci + 1) * c)
        stack = lambda parts: jnp.concatenate([p[sl] for p in parts], axis=0)
        q_s, k_s, v_s = stack(qh), stack(kh), stack(vh)
        g_c = g_all[sl]
        csum = g_c
        for sh in (1, 2, 4, 8, 16, 32):
            csum = csum + jnp.where(row_in_chunk >= sh, pltpu.roll(csum, sh, 0), 0.0)
        total = csum[c - 1:c]
        suffix = total - csum + g_c
        beta_c = beta_all[sl]
        for di in range(2):
            gcum = csum if di == 0 else suffix
            lanes = [GDN_HEADS * di + hd for hd in range(GDN_HEADS)]
            beta_s = jnp.concatenate([_col(beta_c, ln) for ln in lanes], axis=0)
            gc_s = jnp.concatenate([_col(gcum, 8 + ln) for ln in lanes], axis=0)
            gl_s = jnp.concatenate([jnp.broadcast_to(_col(total, 8 + ln), (c, 1)) for ln in lanes], axis=0)
            gc_row = jnp.sum(jnp.where(eye, jnp.broadcast_to(gc_s, (st, st)), 0.0), axis=0, keepdims=True)
            tri = (rr >= cc) if di == 0 else (rr <= cc)
            incl = jnp.logical_and(same, tri)
            strict = jnp.logical_and(incl, jnp.logical_not(eye))
            dec = jnp.where(incl, jnp.exp(jnp.minimum(gc_s - gc_row, 0.0)), 0.0)
            kb_s = k_s * beta_s
            a_mat = jnp.where(strict, _dot_nt(kb_s, k_s) * dec, 0.0)
            att = _dot_nt(q_s, k_s) * dec
            pw = -a_mat
            t_inv = jnp.where(eye, 1.0, 0.0) + pw
            for _ in range(5):
                pw = _dot_hi(pw, pw)
                t_inv = t_inv + _dot_hi(t_inv, pw)
            sol = _dot_hi(t_inv, jnp.concatenate([v_s * beta_s, kb_s * jnp.exp(gc_s)], axis=1))
            u_ref[0, di, ci] = sol[:, :GDN_DK]
            w_ref[0, di, ci] = sol[:, GDN_DK:].astype(BF16)
            qg_ref[0, di, ci] = (q_s * jnp.exp(gc_s)).astype(BF16)
            kdt_ref[0, di, ci] = (k_s * jnp.exp(gl_s - gc_s)).T.astype(BF16)
            att_ref[0, di, ci] = att.astype(BF16)
            gl_row = jnp.concatenate(
                [jnp.broadcast_to(jnp.exp(_col(total, 8 + ln)), (8, GDN_DK)) for ln in lanes], axis=1)
            gl_ref[0, di, ci] = gl_row


def _gdn_prep(ug, gt, conv_w, arow, dtrow, s_len):
    b, r, width = ug.shape
    tm = ROW_TILE
    cpt = tm // GDN_CHUNK
    nc = r // GDN_CHUNK
    st = GDN_STACK
    full = lambda a: pl.BlockSpec(a.shape, lambda bi, i: (0,) * a.ndim)
    out = lambda *tail: pl.BlockSpec((1, 2, cpt) + tail, lambda bi, i: (bi, 0, i) + (0,) * len(tail))
    shp = lambda dt, *tail: jax.ShapeDtypeStruct((b, 2, nc) + tail, dt)
    return pl.pallas_call(
        functools.partial(_gdn_prep_kernel, s_len=s_len, r_len=r),
        grid=(b, r // tm),
        in_specs=_halo_specs(tm, width, lambda bi, i: i, r)
        + [pl.BlockSpec((1, tm, 128), lambda bi, i: (bi, i, 0)), full(conv_w), full(arow), full(dtrow)],
        out_specs=[out(st, GDN_DK), out(st, GDN_DK), out(st, GDN_DK), out(GDN_DK, st), out(st, st),
                   out(8, GDN_HEADS * GDN_DK)],
        out_shape=[shp(BF16, st, GDN_DK), shp(F32, st, GDN_DK), shp(BF16, st, GDN_DK), shp(BF16, GDN_DK, st),
                   shp(BF16, st, st), shp(F32, 8, GDN_HEADS * GDN_DK)],
        scratch_shapes=[pltpu.VMEM((tm + 16, width), F32)],
        compiler_params=_params("parallel", "parallel"),
        name="gdn_prep",
    )(ug, ug, ug, gt, conv_w, arow, dtrow)


def _gdn_scan_kernel(*refs):
    ins, o_refs, s_ref = refs[:12], refs[12:14], refs[14]
    c = GDN_CHUNK
    dk = GDN_DK
    st = GDN_STACK

    @pl.when(pl.program_id(1) == 0)
    def _():
        s_ref[...] = jnp.zeros(s_ref.shape, F32)

    rr = lax.broadcasted_iota(jnp.int32, (st, GDN_HEADS * dk), 0) // c
    cc = lax.broadcasted_iota(jnp.int32, (st, GDN_HEADS * dk), 1) // dk
    block = rr == cc
    diag = lambda m: jnp.concatenate([m[hd * c:(hd + 1) * c, hd * dk:(hd + 1) * dk] for hd in range(GDN_HEADS)],
                                     axis=0)
    for di in range(2):
        w_ref, u_ref, qg_ref, kdt_ref, att_ref, gl_ref = ins[6 * di:6 * di + 6]
        s_all = s_ref[di]
        r1 = _dot(jnp.concatenate([w_ref[0, 0, 0], qg_ref[0, 0, 0]], axis=0), s_all)
        v_new = u_ref[0, 0, 0] - diag(r1[:st])
        v_bd = jnp.where(block, jnp.concatenate([v_new] * GDN_HEADS, axis=1), 0.0)
        r2 = _dot(jnp.concatenate([att_ref[0, 0, 0], kdt_ref[0, 0, 0]], axis=0), v_bd)
        o_s = diag(r1[st:]) + diag(r2[:st])
        o_refs[di][0] = jnp.concatenate([o_s[hd * c:(hd + 1) * c] for hd in range(GDN_HEADS)], axis=1)
        s_ref[di] = s_all * gl_ref[0, 0, 0, 0:1, :] + r2[st:]


def _gdn_scan(prep, s_len):
    w, u, qg, kdt, att, gl = prep
    b, _, nc = w.shape[:3]
    nc_lat = s_len // GDN_CHUNK
    nc_ctx = nc - nc_lat

    def chunk(di, n):
        if di == 0:
            return jnp.where(n < nc_ctx, nc_lat + n, n - nc_ctx)
        return nc - 1 - n

    def spec(a, di):
        tail = a.shape[3:]
        return pl.BlockSpec((1, 1, 1) + tail, lambda bi, n: (bi, di, chunk(di, n)) + (0,) * len(tail))

    hv = GDN_HEADS * GDN_DK
    r = nc * GDN_CHUNK
    return pl.pallas_call(
        _gdn_scan_kernel,
        grid=(b, nc),
        in_specs=[spec(a, di) for di in range(2) for a in (w, u, qg, kdt, att, gl)],
        out_specs=[pl.BlockSpec((1, GDN_CHUNK, hv), lambda bi, n, di=di: (bi, chunk(di, n), 0)) for di in range(2)],
        out_shape=[jax.ShapeDtypeStruct((b, r, hv), F32)] * 2,
        scratch_shapes=[pltpu.VMEM((2, GDN_DK, hv), F32)],
        compiler_params=_params("parallel", "arbitrary"),
        name="gdn_scan",
    )(*([w, u, qg, kdt, att, gl] * 2))


def _post_kernel(*refs, gdn_mix, hidden, th):
    if gdn_mix:
        h_ref, mod_ref, ax_ref, of_ref, ob_ref, z_ref, on_ref, n2_ref, wo_ref, wgu_ref, wd_ref, o_ref = refs
    else:
        h_ref, mod_ref, mix_ref, n2_ref, wo_ref, wgu_ref, wd_ref, o_ref = refs
    mod = mod_ref[0, 0]
    gate1, shift2, scale2, gate2 = mod[2:3], mod[3:4], mod[4:5], mod[5:6]
    if gdn_mix:
        o = of_ref[0] + ob_ref[0]
        parts = []
        for hd in range(GDN_HEADS):
            oh = o[:, hd * GDN_DK:(hd + 1) * GDN_DK]
            parts.append(oh * lax.rsqrt(jnp.mean(oh * oh, axis=-1, keepdims=True) + EPS))
        bx = jnp.concatenate(parts, axis=1) * on_ref[...] * _silu(z_ref[0])
        half = ax_ref.shape[2]
        mixed = _dot(ax_ref[0], wo_ref[0:half, :]) + _dot(bx, wo_ref[half:, :])
    else:
        mixed = _dot(mix_ref[0], wo_ref[...])
    h1 = h_ref[0] + gate1 * mixed
    y = _modulate(h1, n2_ref[...], shift2, scale2).astype(BF16)
    acc = jnp.zeros(h1.shape, F32)
    for j in range(hidden // th):
        g = _dot(y, wgu_ref[:, j * th:(j + 1) * th])
        u = _dot(y, wgu_ref[:, hidden + j * th:hidden + (j + 1) * th])
        acc = acc + _dot(_silu(g) * u, wd_ref[j * th:(j + 1) * th, :])
    o_ref[0] = h1 + gate2 * acc


def _post(h, mod, mix_inputs, norm2, w_out, w_gu, w_down, s_len, rows_out):
    b, _, d = h.shape
    tm = ROW_TILE
    n_lat = s_len // tm
    hidden = w_down.shape[0]
    gdn_mix = len(mix_inputs) > 1
    full = lambda a: pl.BlockSpec(a.shape, lambda bi, i: (0,) * a.ndim)
    rows = lambda w: pl.BlockSpec((1, tm, w), lambda bi, i: (bi, i, 0))
    if gdn_mix:
        ax, o_f, o_b, z, o_gain = mix_inputs
        mix_args = [ax, o_f, o_b, z, o_gain]
        mix_specs = [rows(ax.shape[2]), rows(o_f.shape[2]), rows(o_b.shape[2]), rows(z.shape[2]), full(o_gain)]
    else:
        mix_args = list(mix_inputs)
        mix_specs = [rows(mix_inputs[0].shape[2])]
    return pl.pallas_call(
        functools.partial(_post_kernel, gdn_mix=gdn_mix, hidden=hidden, th=256),
        grid=(b, rows_out // tm),
        in_specs=[rows(d), pl.BlockSpec((1, 1, 8, d), lambda bi, i: (bi, jnp.where(i < n_lat, 0, 1), 0, 0))]
        + mix_specs + [full(norm2), full(w_out), full(w_gu), full(w_down)],
        out_specs=rows(d),
        out_shape=jax.ShapeDtypeStruct((b, rows_out, d), F32),
        compiler_params=_params("parallel", "parallel"),
        name="post_gdn" if gdn_mix else "post_lru",
    )(h, mod, *mix_args, norm2, w_out, w_gu, w_down)


def _inproj1_kernel(h_ref, mod_ref, n1_ref, wa_ref, wb_ref, gl_ref, xr_ref):
    y = _modulate(h_ref[0], n1_ref[...], mod_ref[0, 0, 0:1, :], mod_ref[0, 0, 1:2, :]).astype(BF16)
    gl_ref[0] = jax.nn.gelu(_dot(y, wa_ref[...]))
    xr_ref[0] = _dot(y, wb_ref[...])


def _inproj1(h, mod, norm1, wa, wb, s_len):
    b, r, d = h.shape
    tm = ROW_TILE
    n_lat = s_len // tm
    full = lambda a: pl.BlockSpec(a.shape, lambda bi, i: (0,) * a.ndim)
    rows = lambda w: pl.BlockSpec((1, tm, w), lambda bi, i: (bi, i, 0))
    return pl.pallas_call(
        _inproj1_kernel,
        grid=(b, r // tm),
        in_specs=[rows(d), pl.BlockSpec((1, 1, 8, d), lambda bi, i: (bi, jnp.where(i < n_lat, 0, 1), 0, 0)),
                  full(norm1), full(wa), full(wb)],
        out_specs=[rows(wa.shape[1]), rows(wb.shape[1])],
        out_shape=[jax.ShapeDtypeStruct((b, r, wa.shape[1]), F32), jax.ShapeDtypeStruct((b, r, wb.shape[1]), F32)],
        compiler_params=_params("parallel", "parallel"),
        name="inproj1",
    )(h, mod, norm1, wa, wb)


def _expm1(x):
    return jnp.tanh(0.5 * x) * (jnp.exp(x) + 1.0)


def _lru_scan_kernel(*refs, rev, fuse_mix, s_len, r_len, n_tiles):
    if fuse_mix:
        (prev_ref, x_ref, next_ref, cw_ref, cb_ref, wg_ref, bg_ref, lam_ref, hf_ref, gl_ref,
         o_ref, ext_ref, a_ref, b_ref, carry_ref) = refs
    else:
        (prev_ref, x_ref, next_ref, cw_ref, cb_ref, wg_ref, bg_ref, lam_ref,
         o_ref, ext_ref, a_ref, b_ref, carry_ref) = refs
    n = pl.program_id(1)
    tile = _lru_tile(n, rev, s_len, r_len, n_tiles)
    tm = x_ref.shape[1]
    bw = wg_ref.shape[1]

    @pl.when(n == 0)
    def _():
        carry_ref[...] = jnp.zeros(carry_ref.shape, F32)

    xc = _conv_taps(ext_ref, prev_ref, x_ref, next_ref, cw_ref, tile, s_len, r_len) + cb_ref[...]
    res = [_dot(xc[:, kb * bw:(kb + 1) * bw], wg_ref[kb]) for kb in range(LRU_BLOCKS)]
    r_gate = jax.nn.sigmoid(jnp.concatenate([t[:, :bw] for t in res], axis=1) + bg_ref[0:1])
    i_gate = jax.nn.sigmoid(jnp.concatenate([t[:, bw:] for t in res], axis=1) + bg_ref[1:2])
    log_a = -LRU_C * r_gate * jax.nn.softplus(-lam_ref[...])
    a_ref[...] = jnp.exp(log_a)
    b_ref[...] = jnp.sqrt(-_expm1(2.0 * log_a)) * (i_gate * xc)

    row = lax.broadcasted_iota(jnp.int32, (8, a_ref.shape[1]), 0)
    groups = tm // 8

    def group(gi, carry):
        g0 = pl.multiple_of((groups - 1 - gi if rev else gi) * 8, 8)
        acc_a = a_ref[pl.ds(g0, 8), :]
        acc_b = b_ref[pl.ds(g0, 8), :]
        for sh in (1, 2, 4):
            keep = (row < 8 - sh) if rev else (row >= sh)
            amt = 8 - sh if rev else sh
            sh_b = pltpu.roll(acc_b, amt, 0)
            sh_a = pltpu.roll(acc_a, amt, 0)
            acc_b = jnp.where(keep, acc_a * sh_b + acc_b, acc_b)
            acc_a = jnp.where(keep, acc_a * sh_a, acc_a)
        hs = acc_a * carry + acc_b
        if fuse_mix:
            o_ref[0, pl.ds(g0, 8), :] = (gl_ref[0, pl.ds(g0, 8), :] * (hf_ref[0, pl.ds(g0, 8), :] + hs)).astype(BF16)
        else:
            o_ref[0, pl.ds(g0, 8), :] = hs
        return hs[0:1] if rev else hs[7:8]

    carry_ref[...] = lax.fori_loop(0, groups, group, carry_ref[...])


def _lru_tile(n, rev, s_len, r_len, n_tiles):
    tm = ROW_TILE
    n_lat = s_len // tm
    n_ctx = n_tiles - n_lat
    if rev:
        return n_tiles - 1 - n
    return jnp.where(n < n_ctx, n_lat + n, n - n_ctx)


def _lru_scan(xr, conv_w, conv_b, w_gate, b_gate, lam_row, s_len, rev, h_fwd=None, gelu=None):
    b, r, width = xr.shape
    tm = ROW_TILE
    n_tiles = r // tm
    fuse_mix = h_fwd is not None
    tile_of = lambda bi, n: _lru_tile(n, rev, s_len, r, n_tiles)
    full = lambda a: pl.BlockSpec(a.shape, lambda bi, n: (0,) * a.ndim)
    cur = pl.BlockSpec((1, tm, width), lambda bi, n: (bi, tile_of(bi, n), 0))
    extra_args, extra_specs = ([h_fwd, gelu], [cur, cur]) if fuse_mix else ([], [])
    return pl.pallas_call(
        functools.partial(_lru_scan_kernel, rev=rev, fuse_mix=fuse_mix, s_len=s_len, r_len=r, n_tiles=n_tiles),
        grid=(b, n_tiles),
        in_specs=_halo_specs(tm, width, tile_of, r)
        + [full(conv_w), full(conv_b), full(w_gate), full(b_gate), full(lam_row)] + extra_specs,
        out_specs=cur,
        out_shape=jax.ShapeDtypeStruct((b, r, width), BF16 if fuse_mix else F32),
        scratch_shapes=[pltpu.VMEM((tm + 16, width), F32), pltpu.VMEM((tm, width), F32),
                        pltpu.VMEM((tm, width), F32), pltpu.VMEM((1, width), F32)],
        compiler_params=_params("parallel", "arbitrary"),
        name="lru_bwd" if rev else "lru_fwd",
    )(xr, xr, xr, conv_w, conv_b, w_gate, b_gate, lam_row, *extra_args)


def _rope_tables(s_len):
    t = jnp.arange(s_len)
    inv = ROPE_THETA ** (-jnp.arange(ROPE_PAIRS, dtype=F32) / ROPE_PAIRS)
    ang_r = (t // GRID_W).astype(F32)[:, None] * inv
    ang_c = (t % GRID_W).astype(F32)[:, None] * inv
    cos = jnp.concatenate([jnp.cos(ang_r)] * 2 + [jnp.cos(ang_c)] * 2, axis=1)
    sin = jnp.concatenate([-jnp.sin(ang_r), jnp.sin(ang_r), -jnp.sin(ang_c), jnp.sin(ang_c)], axis=1)
    return jnp.tile(cos, (1, 2)), jnp.tile(sin, (1, 2))


def kernel(x, c, ctx, c_ctx, ev_norm1, ev_norm2, ev_ada_w, ev_ada_b, ev_w_in, ev_w_out, ev_q_norm, ev_k_norm, ev_lam_q1, ev_lam_k1, ev_lam_q2, ev_lam_k2, ev_sub_norm, ev_gdn_conv, ev_gdn_a_log, ev_gdn_dt_bias, ev_gdn_o_norm, ev_ffn_w_gu, ev_ffn_w_down, od_norm1, od_norm2, od_ada_w, od_ada_b, od_w_in, od_conv_w, od_conv_b, od_w_r, od_b_r, od_w_i, od_b_i, od_lam, od_w_out, od_ffn_w_gu, od_ffn_w_down):
    batch, s_len, d = x.shape
    assert ev_norm1.shape[0] == 1 and od_norm1.shape[0] == 1, "two-layer block: one even and one odd layer"
    assert ctx.shape[1] % ROW_TILE == 0 and s_len % ROW_TILE == 0 and batch + 1 <= 8
    row = lambda v: v.reshape(1, -1).astype(F32)

    h = jnp.concatenate([x, ctx], axis=1)
    cond = jnp.zeros((8, d), F32).at[:batch].set(c).at[batch].set(c_ctx)

    mod0 = _mod_table(_ada_mod(cond, ev_ada_w[0], ev_ada_b[0]), batch, d)
    w_in = ev_w_in[0].astype(BF16)
    qw = DA_HEADS * 2 * DA_QK
    vw = DA_HEADS * DA_V
    gw = GDN_HEADS * GDN_DK
    o_da, o_gdn = 2 * qw + vw, 2 * qw + vw + 3 * gw
    wq, wk, wv = w_in[:, :qw], w_in[:, qw:2 * qw], w_in[:, 2 * qw:o_da]
    wg, wz = w_in[:, o_da:o_gdn], w_in[:, o_gdn:o_gdn + gw]
    wgt = jnp.pad(w_in[:, o_gdn + gw:], ((0, 0), (0, 128 - 4 * GDN_HEADS)))
    grp = jnp.arange(qw) // DA_QK
    gsum = (grp[:, None] == grp[None, :]).astype(BF16)
    qgain = jnp.tile(ev_q_norm[0], qw // DA_QK).reshape(1, qw)
    kgain = jnp.tile(ev_k_norm[0], qw // DA_QK).reshape(1, qw)
    cos, sin = _rope_tables(s_len)
    qz, k, vt, ug, z, gt = _inproj0(h, mod0, row(ev_norm1[0]), wq, wk, wv, wg, wz, wgt, gsum, qgain, kgain,
                                    cos, sin, s_len)
    lambda_init = 0.8 - 0.6 * math.exp(-0.3 * 0)
    lam_rows = jnp.pad(jnp.stack([ev_lam_q1[0], ev_lam_k1[0], ev_lam_q2[0], ev_lam_k2[0]]),
                       ((0, 0), (0, 128 - DA_QK)))
    ax = _diff_attention(qz, k, vt, ev_sub_norm[0].reshape(DA_V, 1), lam_rows, s_len, lambda_init)
    pad16 = lambda v: jnp.zeros((1, 128), F32).at[0, 8:8 + 2 * GDN_HEADS].set(v.reshape(-1))
    prep = _gdn_prep(ug, gt, ev_gdn_conv[0], pad16(ev_gdn_a_log[0]), pad16(ev_gdn_dt_bias[0]), s_len)
    o_f, o_b = _gdn_scan(prep, s_len)
    o_gain = jnp.tile(ev_gdn_o_norm[0], GDN_HEADS).reshape(1, gw)
    h = _post(h, mod0, (ax, o_f, o_b, z, o_gain), row(ev_norm2[0]), ev_w_out[0].astype(BF16),
              ev_ffn_w_gu[0].astype(BF16), ev_ffn_w_down[0].astype(BF16), s_len, h.shape[1])

    mod1 = _mod_table(_ada_mod(cond, od_ada_w[0], od_ada_b[0]), batch, d)
    w_in1 = od_w_in[0].astype(BF16)
    lw = od_conv_b.shape[1]
    gelu, xr = _inproj1(h, mod1, row(od_norm1[0]), w_in1[:, :lw], w_in1[:, lw:], s_len)
    w_gate = jnp.concatenate([od_w_r[0], od_w_i[0]], axis=-1).astype(BF16)
    b_gate = jnp.stack([od_b_r[0], od_b_i[0]], axis=1)
    h_f = _lru_scan(xr, od_conv_w[0], row(od_conv_b[0]), w_gate[0], b_gate[0], row(od_lam[0, 0]), s_len, False)
    mix = _lru_scan(xr, od_conv_w[0], row(od_conv_b[0]), w_gate[1], b_gate[1], row(od_lam[0, 1]), s_len, True,
                    h_fwd=h_f, gelu=gelu)
    return _post(h, mod1, (mix,), row(od_norm2[0]), od_w_out[0].astype(BF16), od_ffn_w_gu[0].astype(BF16),
                 od_ffn_w_down[0].astype(BF16), s_len, s_len)
```

```python
import functools
import math

import jax
import jax.numpy as jnp
from jax import lax
from jax.experimental import pallas as pl
from jax.experimental.pallas import tpu as pltpu

F32 = jnp.float32
BF16 = jnp.bfloat16
EPS = 1e-6
HIGHEST = lax.Precision.HIGHEST

GRID_W = 64
ROPE_THETA = 10000.0
DA_HEADS = 4
DA_QK = 64
DA_V = 128
DA_SCALE = DA_QK ** -0.5
ROPE_PAIRS = DA_QK // 4
GDN_HEADS = 4
GDN_DK = 128
GDN_CHUNK = 64
GDN_STACK = GDN_HEADS * GDN_CHUNK
LRU_BLOCKS = 8
LRU_C = 8.0
LOG2E = math.log2(math.e)

ROW_TILE = 256
VMEM_LIMIT = 56 * 1024 * 1024


def _mxu(a, b):
    return jnp.dot(a, b, preferred_element_type=F32)


def _mxu_nt(a, b):
    return lax.dot_general(a, b, (((1,), (1,)), ((), ())), preferred_element_type=F32)


def _dot(a, b):
    return _mxu(a.astype(BF16), b.astype(BF16))


def _dot_nt(a, b):
    return _mxu_nt(a.astype(BF16), b.astype(BF16))


def _dot_hi(a, b):
    return jnp.dot(a, b, precision=HIGHEST, preferred_element_type=F32)


def _split(a):
    hi = a.astype(BF16)
    return hi, (a - hi.astype(F32)).astype(BF16)


def _dot3(a, b):
    a_hi, a_lo = _split(a)
    b_hi, b_lo = _split(b)
    return _mxu(a_hi, b_hi) + (_mxu(a_hi, b_lo) + _mxu(a_lo, b_hi))


def _silu(x):
    return x * jax.nn.sigmoid(x)


def _modulate(x, gain, shift, scale):
    ms = jnp.mean(x * x, axis=-1, keepdims=True)
    return x * lax.rsqrt(ms + EPS) * gain * (1.0 + scale) + shift


def _params(*sem):
    return pltpu.CompilerParams(dimension_semantics=sem, vmem_limit_bytes=VMEM_LIMIT)


def _ada_kernel(c_ref, w_ref, b_ref, o_ref):
    o_ref[...] = _dot_hi(_silu(c_ref[...]), w_ref[...]) + b_ref[...]


def _ada_mod(cond, w, b):
    d = w.shape[0]
    n = w.shape[1]
    return pl.pallas_call(
        _ada_kernel,
        grid=(n // d,),
        in_specs=[pl.BlockSpec((8, d), lambda j: (0, 0)),
                  pl.BlockSpec((d, d), lambda j: (0, j)),
                  pl.BlockSpec((1, d), lambda j: (0, j))],
        out_specs=pl.BlockSpec((8, d), lambda j: (0, j)),
        out_shape=jax.ShapeDtypeStruct((8, n), F32),
        compiler_params=_params("parallel"),
        name="ada_mod",
    )(cond, w, b.reshape(1, n))


def _mod_table(m, batch, d):
    six = m.reshape(8, 6, d)
    lat = six[:batch]
    ctx = jnp.broadcast_to(six[batch][None], (batch, 6, d))
    t = jnp.stack([lat, ctx], axis=1)
    return jnp.pad(t, ((0, 0), (0, 0), (0, 2), (0, 0)))


def _rope(x, cos, sin_signed):
    n = x.shape[-1]
    lane = lax.broadcasted_iota(jnp.int32, x.shape, 1) % (2 * ROPE_PAIRS)
    partner = jnp.where(lane < ROPE_PAIRS, pltpu.roll(x, n - ROPE_PAIRS, 1), pltpu.roll(x, ROPE_PAIRS, 1))
    return x * cos + partner * sin_signed


def _inproj0_kernel(h_ref, mod_ref, n1_ref, wq_ref, wk_ref, wv_ref, wg_ref, wz_ref, wgt_ref, gsum_ref,
                    qg_ref, kg_ref, cos_ref, sin_ref,
                    qz_ref, k_ref, vt_ref, ug_ref, z_ref, gt_ref, *, n_lat_tiles):
    is_lat = pl.program_id(1) < n_lat_tiles
    y = _modulate(h_ref[0], n1_ref[...], mod_ref[0, 0, 0:1, :], mod_ref[0, 0, 1:2, :])
    yb = y.astype(BF16)
    cos = jnp.concatenate([cos_ref[...]] * DA_HEADS, axis=1)
    sin = jnp.concatenate([sin_ref[...]] * DA_HEADS, axis=1)

    def head_norm_rope(w_ref, gain_ref):
        u = _dot(yb, w_ref[...])
        sq = u * u
        hi = sq.astype(BF16)
        lo = (sq - hi.astype(F32)).astype(BF16)
        ss = _dot(hi, gsum_ref[...]) + _dot(lo, gsum_ref[...])
        un = u * lax.rsqrt(ss * (1.0 / DA_QK) + EPS) * gain_ref[...]
        return jnp.where(is_lat, _rope(un, cos, sin), un)

    q = head_norm_rope(wq_ref, qg_ref) * (DA_SCALE * LOG2E)
    k = head_norm_rope(wk_ref, kg_ref)
    low = lax.broadcasted_iota(jnp.int32, (q.shape[0], DA_V), 1) < DA_QK
    parts = []
    for hd in range(DA_HEADS):
        qh = q[:, hd * DA_V:(hd + 1) * DA_V]
        parts += [jnp.where(low, qh, 0.0), jnp.where(low, 0.0, qh)]
    qz_ref[0] = jnp.concatenate(parts, axis=1).astype(BF16)
    k_ref[0] = k.astype(BF16)
    vt_ref[0] = _dot(yb, wv_ref[...]).T.astype(BF16)
    ug_ref[0] = _dot(yb, wg_ref[...])
    z_ref[0] = _dot(yb, wz_ref[...])
    gt_ref[0] = _dot(yb, wgt_ref[...])


def _inproj0(h, mod, norm1, wq, wk, wv, wg, wz, wgt, gsum, qgain, kgain, cos, sin, s_len):
    b, r, d = h.shape
    tm = ROW_TILE
    n_lat = s_len // tm
    full = lambda a: pl.BlockSpec(a.shape, lambda bi, i: (0,) * a.ndim)
    rows = lambda w: pl.BlockSpec((1, tm, w), lambda bi, i: (bi, i, 0))
    tab = pl.BlockSpec((tm, 128), lambda bi, i: (jnp.minimum(i, n_lat - 1), 0))
    return pl.pallas_call(
        functools.partial(_inproj0_kernel, n_lat_tiles=n_lat),
        grid=(b, r // tm),
        in_specs=[rows(d),
                  pl.BlockSpec((1, 1, 8, d), lambda bi, i: (bi, jnp.where(i < n_lat, 0, 1), 0, 0)),
                  full(norm1), full(wq), full(wk), full(wv), full(wg), full(wz), full(wgt), full(gsum),
                  full(qgain), full(kgain), tab, tab],
        out_specs=[rows(2 * DA_HEADS * DA_V), rows(DA_HEADS * DA_V),
                   pl.BlockSpec((1, DA_HEADS * DA_V, tm), lambda bi, i: (bi, 0, i)),
                   rows(wg.shape[1]), rows(wz.shape[1]), rows(128)],
        out_shape=[jax.ShapeDtypeStruct((b, r, 2 * DA_HEADS * DA_V), BF16),
                   jax.ShapeDtypeStruct((b, r, DA_HEADS * DA_V), BF16),
                   jax.ShapeDtypeStruct((b, DA_HEADS * DA_V, r), BF16),
                   jax.ShapeDtypeStruct((b, r, wg.shape[1]), F32),
                   jax.ShapeDtypeStruct((b, r, wz.shape[1]), F32),
                   jax.ShapeDtypeStruct((b, r, 128), F32)],
        compiler_params=_params("parallel", "parallel"),
        name="inproj0",
    )(h, mod, norm1, wq, wk, wv, wg, wz, wgt, gsum, qgain, kgain, cos, sin)


def _attn_kernel(*refs, tk, n_main, tail, lambda_init):
    qz_ref, k_ref, vt_ref, sub_ref, lam_ref = refs[:5]
    o_ref, acc_ref, m_ref, l_ref, s_ref = refs[-5:]
    m_ref[...] = jnp.full(m_ref.shape, -1e30, F32)
    l_ref[...] = jnp.zeros(l_ref.shape, F32)
    acc_ref[...] = jnp.zeros(acc_ref.shape, F32)
    qs = (qz_ref[0, :, 0:DA_V], qz_ref[0, :, DA_V:2 * DA_V])

    def scores(kc, mi):
        return _mxu_nt(kc, qs[mi])

    def softmax_pv(st, vc, mi):
        m_old = m_ref[mi]
        m_new = jnp.maximum(m_old, jnp.max(st, axis=0, keepdims=True))
        alpha = jnp.exp2(m_old - m_new)
        p = jnp.exp2(st - m_new)
        l_ref[mi] = l_ref[mi] * alpha + jnp.sum(p, axis=0, keepdims=True)
        acc_ref[mi] = acc_ref[mi] * alpha + _mxu(vc, p.astype(BF16))
        m_ref[mi] = m_new

    def qk(j, slot):
        kc = k_ref[0, pl.ds(pl.multiple_of(j * tk, tk), tk), :]
        for mi in range(2):
            s_ref[slot, mi] = scores(kc, mi)

    def consume(j, slot):
        vc = vt_ref[0, :, pl.ds(pl.multiple_of(j * tk, tk), tk)]
        for mi in range(2):
            softmax_pv(s_ref[slot, mi], vc, mi)

    def body(jj, carry):
        j = 2 * jj
        qk(j + 1, 1)
        consume(j, 0)
        qk(jnp.minimum(j + 2, n_main - 1), 0)
        consume(j + 1, 1)
        return carry

    if n_main:
        qk(0, 0)
        lax.fori_loop(0, n_main // 2, body, 0)
    if tail is not None:
        t0, tn = tail
        kc = k_ref[0, t0:t0 + tn, :]
        vc = vt_ref[0, :, t0:t0 + tn]
        for mi in range(2):
            softmax_pv(scores(kc, mi), vc, mi)

    lam_rows = lam_ref[...]
    e1 = jnp.exp(jnp.sum(lam_rows[0:1] * lam_rows[1:2], axis=-1, keepdims=True))
    e2 = jnp.exp(jnp.sum(lam_rows[2:3] * lam_rows[3:4], axis=-1, keepdims=True))
    lam = e1 - e2 + lambda_init
    o = acc_ref[0] / l_ref[0] - lam * (acc_ref[1] / l_ref[1])
    ms = jnp.mean(o * o, axis=0, keepdims=True)
    y = o * lax.rsqrt(ms + EPS) * sub_ref[...] * (1.0 - lambda_init)
    o_ref[0] = y.T.astype(BF16)


ATTN_TQ = 512
ATTN_TK = 512


def _attn_call(qz, k, vt, sub_col, lam_rows, lambda_init, *, tq, q_tile0, n_q, key_rows, key_tile0, n_main, tail, name):
    b = k.shape[0]
    assert n_main % 2 == 0
    kern = functools.partial(_attn_kernel, tk=ATTN_TK, n_main=n_main, tail=tail, lambda_init=lambda_init)
    return pl.pallas_call(
        kern,
        grid=(b, DA_HEADS, n_q),
        in_specs=[pl.BlockSpec((1, tq, 2 * DA_V), lambda bi, hd, i: (bi, q_tile0 + i, hd)),
                  pl.BlockSpec((1, key_rows, DA_V), lambda bi, hd, i: (bi, key_tile0, hd)),
                  pl.BlockSpec((1, DA_V, key_rows), lambda bi, hd, i: (bi, hd, key_tile0)),
                  pl.BlockSpec((DA_V, 1), lambda bi, hd, i: (0, 0)),
                  pl.BlockSpec((4, 128), lambda bi, hd, i: (0, 0))],
        out_specs=pl.BlockSpec((1, tq, DA_V), lambda bi, hd, i: (bi, i, hd)),
        out_shape=jax.ShapeDtypeStruct((b, n_q * tq, DA_HEADS * DA_V), BF16),
        scratch_shapes=[pltpu.VMEM((2, DA_V, tq), F32), pltpu.VMEM((2, 1, tq), F32),
                        pltpu.VMEM((2, 1, tq), F32), pltpu.VMEM((2, 2, ATTN_TK, tq), F32)],
        compiler_params=_params("parallel", "parallel", "arbitrary"),
        name=name,
    )(qz, k, vt, sub_col, lam_rows)


def _diff_attention(qz, k, vt, sub_col, lam_rows, s_len, lambda_init):
    r = k.shape[1]
    c_len = r - s_len
    assert s_len % ATTN_TQ == 0 and s_len % ATTN_TK == 0 and s_len % c_len == 0 and c_len % 128 == 0
    ax = _attn_call(qz, k, vt, sub_col, lam_rows, lambda_init, tq=ATTN_TQ, q_tile0=0, n_q=s_len // ATTN_TQ,
                    key_rows=r, key_tile0=0, n_main=s_len // ATTN_TK, tail=(s_len, c_len), name="diff_attn")
    ac = _attn_call(qz, k, vt, sub_col, lam_rows, lambda_init, tq=c_len, q_tile0=s_len // c_len, n_q=1,
                    key_rows=c_len, key_tile0=s_len // c_len, n_main=0, tail=(0, c_len), name="diff_attn_ctx")
    return jnp.concatenate([ax, ac], axis=1)


def _conv_taps(ext_ref, prev_ref, x_ref, next_ref, w_ref, tile, s_len, r_len):
    tm = x_ref.shape[1]
    r0 = tile * tm
    has_prev = jnp.logical_and(r0 != 0, r0 != s_len)
    has_next = jnp.logical_and(r0 + tm != s_len, r0 + tm != r_len)
    x = x_ref[0]
    ext_ref[0:8, :] = jnp.where(has_prev, prev_ref[0], 0.0)
    ext_ref[8:8 + tm, :] = x
    ext_ref[8 + tm:16 + tm, :] = jnp.where(has_next, next_ref[0], 0.0)
    w = w_ref[...]
    return (ext_ref[pl.ds(6, tm), :] * w[0:1] + ext_ref[pl.ds(7, tm), :] * w[1:2]
            + x * w[2:3] + ext_ref[pl.ds(9, tm), :] * w[3:4])


def _halo_specs(tm, width, tile_of, n_rows):
    blocks = n_rows // 8
    cur = lambda *g: tile_of(*g)
    return [pl.BlockSpec((1, 8, width), lambda *g: (g[0], jnp.maximum(cur(*g) * (tm // 8) - 1, 0), 0)),
            pl.BlockSpec((1, tm, width), lambda *g: (g[0], cur(*g), 0)),
            pl.BlockSpec((1, 8, width), lambda *g: (g[0], jnp.minimum((cur(*g) + 1) * (tm // 8), blocks - 1), 0))]


def _col(x, lane):
    return x[:, lane:lane + 1]


def _gdn_prep_kernel(prev_ref, x_ref, next_ref, gt_ref, cw_ref, arow_ref, dtrow_ref,
                     w_ref, u_ref, qg_ref, kdt_ref, att_ref, gl_ref, ext_ref, *, s_len, r_len):
    tile = pl.program_id(1)
    tm = x_ref.shape[1]
    hw = GDN_HEADS * GDN_DK
    qkv = _silu(_conv_taps(ext_ref, prev_ref, x_ref, next_ref, cw_ref, tile, s_len, r_len))

    def l2n(a):
        return a * lax.rsqrt(jnp.sum(a * a, axis=-1, keepdims=True) + EPS)

    qh = [l2n(qkv[:, hd * GDN_DK:(hd + 1) * GDN_DK]) * GDN_DK ** -0.5 for hd in range(GDN_HEADS)]
    kh = [l2n(qkv[:, hw + hd * GDN_DK:hw + (hd + 1) * GDN_DK]) for hd in range(GDN_HEADS)]
    vh = [qkv[:, 2 * hw + hd * GDN_DK:2 * hw + (hd + 1) * GDN_DK] for hd in range(GDN_HEADS)]

    gt = gt_ref[0]
    beta_all = jax.nn.sigmoid(gt)
    g_all = -jnp.exp(arow_ref[...]) * jax.nn.softplus(gt + dtrow_ref[...])

    c = GDN_CHUNK
    st = GDN_STACK
    rr = lax.broadcasted_iota(jnp.int32, (st, st), 0)
    cc = lax.broadcasted_iota(jnp.int32, (st, st), 1)
    same = (rr // c) == (cc // c)
    eye = rr == cc
    row_in_chunk = lax.broadcasted_iota(jnp.int32, (c, 128), 0)

    for ci in range(tm // c):
        sl = slice(ci * c, (ci + 1) * c)
        stack = lambda parts: jnp.concatenate([p[sl] for p in parts], axis=0)
        q_s, k_s, v_s = stack(qh), stack(kh), stack(vh)
        g_c = g_all[sl]
        csum = g_c
        for sh in (1, 2, 4, 8, 16, 32):
            csum = csum + jnp.where(row_in_chunk >= sh, pltpu.roll(csum, sh, 0), 0.0)
        total = csum[c - 1:c]
        suffix = total - csum + g_c
        beta_c = beta_all[sl]
        for di in range(2):
            gcum = csum if di == 0 else suffix
            lanes = [GDN_HEADS * di + hd for hd in range(GDN_HEADS)]
            beta_s = jnp.concatenate([_col(beta_c, ln) for ln in lanes], axis=0)
            gc_s = jnp.concatenate([_col(gcum, 8 + ln) for ln in lanes], axis=0)
            gl_s = jnp.concatenate([jnp.broadcast_to(_col(total, 8 + ln), (c, 1)) for ln in lanes], axis=0)
            gc_row = jnp.sum(jnp.where(eye, jnp.broadcast_to(gc_s, (st, st)), 0.0), axis=0, keepdims=True)
            tri = (rr >= cc) if di == 0 else (rr <= cc)
            incl = jnp.logical_and(same, tri)
            strict = jnp.logical_and(incl, jnp.logical_not(eye))
            dec = jnp.where(incl, jnp.exp(jnp.minimum(gc_s - gc_row, 0.0)), 0.0)
            kb_s = k_s * beta_s
            a_mat = jnp.where(strict, _dot_nt(kb_s, k_s) * dec, 0.0)
            att = _dot_nt(q_s, k_s) * dec
            ident = jnp.where(eye, 1.0, 0.0)
            pw = -a_mat
            t_inv = ident + pw
            for _ in range(5):
                pw = _dot(pw, pw)
                t_inv = t_inv + _dot(t_inv, pw)
            resid = ident - t_inv - _dot3(a_mat, t_inv)
            t_inv = t_inv + _dot(t_inv, resid)
            sol = _dot3(t_inv, jnp.concatenate([v_s * beta_s, kb_s * jnp.exp(gc_s)], axis=1))
            u_ref[0, di, ci] = sol[:, :GDN_DK]
            w_ref[0, di, ci] = sol[:, GDN_DK:].astype(BF16)
            qg_ref[0, di, ci] = (q_s * jnp.exp(gc_s)).astype(BF16)
            kdt_ref[0, di, ci] = (k_s * jnp.exp(gl_s - gc_s)).T.astype(BF16)
            att_ref[0, di, ci] = att.astype(BF16)
            gl_row = jnp.concatenate(
                [jnp.broadcast_to(jnp.exp(_col(total, 8 + ln)), (8, GDN_DK)) for ln in lanes], axis=1)
            gl_ref[0, di, ci] = gl_row


def _gdn_prep(ug, gt, conv_w, arow, dtrow, s_len):
    b, r, width = ug.shape
    tm = ROW_TILE
    cpt = tm // GDN_CHUNK
    nc = r // GDN_CHUNK
    st = GDN_STACK
    full = lambda a: pl.BlockSpec(a.shape, lambda bi, i: (0,) * a.ndim)
    out = lambda *tail: pl.BlockSpec((1, 2, cpt) + tail, lambda bi, i: (bi, 0, i) + (0,) * len(tail))
    shp = lambda dt, *tail: jax.ShapeDtypeStruct((b, 2, nc) + tail, dt)
    return pl.pallas_call(
        functools.partial(_gdn_prep_kernel, s_len=s_len, r_len=r),
        grid=(b, r // tm),
        in_specs=_halo_specs(tm, width, lambda bi, i: i, r)
        + [pl.BlockSpec((1, tm, 128), lambda bi, i: (bi, i, 0)), full(conv_w), full(arow), full(dtrow)],
        out_specs=[out(st, GDN_DK), out(st, GDN_DK), out(st, GDN_DK), out(GDN_DK, st), out(st, st),
                   out(8, GDN_HEADS * GDN_DK)],
        out_shape=[shp(BF16, st, GDN_DK), shp(F32, st, GDN_DK), shp(BF16, st, GDN_DK), shp(BF16, GDN_DK, st),
                   shp(BF16, st, st), shp(F32, 8, GDN_HEADS * GDN_DK)],
        scratch_shapes=[pltpu.VMEM((tm + 16, width), F32)],
        compiler_params=_params("parallel", "parallel"),
        name="gdn_prep",
    )(ug, ug, ug, gt, conv_w, arow, dtrow)


def _gdn_scan_kernel(*refs):
    ins, o_refs, s_ref = refs[:12], refs[12:14], refs[14]
    c = GDN_CHUNK
    dk = GDN_DK
    st = GDN_STACK

    @pl.when(pl.program_id(1) == 0)
    def _():
        s_ref[...] = jnp.zeros(s_ref.shape, F32)

    rr = lax.broadcasted_iota(jnp.int32, (st, GDN_HEADS * dk), 0) // c
    cc = lax.broadcasted_iota(jnp.int32, (st, GDN_HEADS * dk), 1) // dk
    block = rr == cc
    diag = lambda m: jnp.concatenate([m[hd * c:(hd + 1) * c, hd * dk:(hd + 1) * dk] for hd in range(GDN_HEADS)],
                                     axis=0)
    for di in range(2):
        w_ref, u_ref, qg_ref, kdt_ref, att_ref, gl_ref = ins[6 * di:6 * di + 6]
        s_all = s_ref[di]
        r1 = _dot(jnp.concatenate([w_ref[0, 0, 0], qg_ref[0, 0, 0]], axis=0), s_all)
        v_new = u_ref[0, 0, 0] - diag(r1[:st])
        v_bd = jnp.where(block, jnp.concatenate([v_new] * GDN_HEADS, axis=1), 0.0)
        r2 = _dot(jnp.concatenate([att_ref[0, 0, 0], kdt_ref[0, 0, 0]], axis=0), v_bd)
        o_s = diag(r1[st:]) + diag(r2[:st])
        o_refs[di][0] = jnp.concatenate([o_s[hd * c:(hd + 1) * c] for hd in range(GDN_HEADS)], axis=1)
        s_ref[di] = s_all * gl_ref[0, 0, 0, 0:1, :] + r2[st:]


def _gdn_scan(prep, s_len):
    w, u, qg, kdt, att, gl = prep
    b, _, nc = w.shape[:3]
    nc_lat = s_len // GDN_CHUNK
    nc_ctx = nc - nc_lat

    def chunk(di, n):
        if di == 0:
            return jnp.where(n < nc_ctx, nc_lat + n, n - nc_ctx)
        return nc - 1 - n

    def spec(a, di):
        tail = a.shape[3:]
        return pl.BlockSpec((1, 1, 1) + tail, lambda bi, n: (bi, di, chunk(di, n)) + (0,) * len(tail))

    hv = GDN_HEADS * GDN_DK
    r = nc * GDN_CHUNK
    return pl.pallas_call(
        _gdn_scan_kernel,
        grid=(b, nc),
        in_specs=[spec(a, di) for di in range(2) for a in (w, u, qg, kdt, att, gl)],
        out_specs=[pl.BlockSpec((1, GDN_CHUNK, hv), lambda bi, n, di=di: (bi, chunk(di, n), 0)) for di in range(2)],
        out_shape=[jax.ShapeDtypeStruct((b, r, hv), F32)] * 2,
        scratch_shapes=[pltpu.VMEM((2, GDN_DK, hv), F32)],
        compiler_params=_params("parallel", "arbitrary"),
        name="gdn_scan",
    )(*([w, u, qg, kdt, att, gl] * 2))


def _post_kernel(*refs, gdn_mix, hidden, th):
    if gdn_mix:
        h_ref, mod_ref, ax_ref, of_ref, ob_ref, z_ref, on_ref, n2_ref, wo_ref, wgu_ref, wd_ref, o_ref = refs
    else:
        h_ref, mod_ref, mix_ref, n2_ref, wo_ref, wgu_ref, wd_ref, o_ref = refs
    mod = mod_ref[0, 0]
    gate1, shift2, scale2, gate2 = mod[2:3], mod[3:4], mod[4:5], mod[5:6]
    if gdn_mix:
        o = of_ref[0] + ob_ref[0]
        parts = []
        for hd in range(GDN_HEADS):
            oh = o[:, hd * GDN_DK:(hd + 1) * GDN_DK]
            parts.append(oh * lax.rsqrt(jnp.mean(oh * oh, axis=-1, keepdims=True) + EPS))
        bx = jnp.concatenate(parts, axis=1) * on_ref[...] * _silu(z_ref[0])
        half = ax_ref.shape[2]
        mixed = _dot(ax_ref[0], wo_ref[0:half, :]) + _dot(bx, wo_ref[half:, :])
    else:
        mixed = _dot(mix_ref[0], wo_ref[...])
    h1 = h_ref[0] + gate1 * mixed
    y = _modulate(h1, n2_ref[...], shift2, scale2).astype(BF16)
    acc = jnp.zeros(h1.shape, F32)
    for j in range(hidden // th):
        g = _dot(y, wgu_ref[:, j * th:(j + 1) * th])
        u = _dot(y, wgu_ref[:, hidden + j * th:hidden + (j + 1) * th])
        acc = acc + _dot(_silu(g) * u, wd_ref[j * th:(j + 1) * th, :])
    o_ref[0] = h1 + gate2 * acc


def _post(h, mod, mix_inputs, norm2, w_out, w_gu, w_down, s_len, rows_out):
    b, _, d = h.shape
    tm = ROW_TILE
    n_lat = s_len // tm
    hidden = w_down.shape[0]
    gdn_mix = len(mix_inputs) > 1
    full = lambda a: pl.BlockSpec(a.shape, lambda bi, i: (0,) * a.ndim)
    rows = lambda w: pl.BlockSpec((1, tm, w), lambda bi, i: (bi, i, 0))
    if gdn_mix:
        ax, o_f, o_b, z, o_gain = mix_inputs
        mix_args = [ax, o_f, o_b, z, o_gain]
        mix_specs = [rows(ax.shape[2]), rows(o_f.shape[2]), rows(o_b.shape[2]), rows(z.shape[2]), full(o_gain)]
    else:
        mix_args = list(mix_inputs)
        mix_specs = [rows(mix_inputs[0].shape[2])]
    return pl.pallas_call(
        functools.partial(_post_kernel, gdn_mix=gdn_mix, hidden=hidden, th=256),
        grid=(b, rows_out // tm),
        in_specs=[rows(d), pl.BlockSpec((1, 1, 8, d), lambda bi, i: (bi, jnp.where(i < n_lat, 0, 1), 0, 0))]
        + mix_specs + [full(norm2), full(w_out), full(w_gu), full(w_down)],
        out_specs=rows(d),
        out_shape=jax.ShapeDtypeStruct((b, rows_out, d), F32),
        compiler_params=_params("parallel", "parallel"),
        name="post_gdn" if gdn_mix else "post_lru",
    )(h, mod, *mix_args, norm2, w_out, w_gu, w_down)


def _inproj1_kernel(h_ref, mod_ref, n1_ref, wa_ref, wb_ref, gl_ref, xr_ref):
    y = _modulate(h_ref[0], n1_ref[...], mod_ref[0, 0, 0:1, :], mod_ref[0, 0, 1:2, :]).astype(BF16)
    gl_ref[0] = jax.nn.gelu(_dot(y, wa_ref[...]))
    xr_ref[0] = _dot(y, wb_ref[...])


def _inproj1(h, mod, norm1, wa, wb, s_len):
    b, r, d = h.shape
    tm = ROW_TILE
    n_lat = s_len // tm
    full = lambda a: pl.BlockSpec(a.shape, lambda bi, i: (0,) * a.ndim)
    rows = lambda w: pl.BlockSpec((1, tm, w), lambda bi, i: (bi, i, 0))
    return pl.pallas_call(
        _inproj1_kernel,
        grid=(b, r // tm),
        in_specs=[rows(d), pl.BlockSpec((1, 1, 8, d), lambda bi, i: (bi, jnp.where(i < n_lat, 0, 1), 0, 0)),
                  full(norm1), full(wa), full(wb)],
        out_specs=[rows(wa.shape[1]), rows(wb.shape[1])],
        out_shape=[jax.ShapeDtypeStruct((b, r, wa.shape[1]), F32), jax.ShapeDtypeStruct((b, r, wb.shape[1]), F32)],
        compiler_params=_params("parallel", "parallel"),
        name="inproj1",
    )(h, mod, norm1, wa, wb)


def _expm1(x):
    return jnp.tanh(0.5 * x) * (jnp.exp(x) + 1.0)


def _lru_scan_kernel(*refs, rev, fuse_mix, s_len, r_len, n_tiles):
    if fuse_mix:
        (prev_ref, x_ref, next_ref, cw_ref, cb_ref, wg_ref, bg_ref, lam_ref, hf_ref, gl_ref,
         o_ref, ext_ref, a_ref, b_ref, carry_ref) = refs
    else:
        (prev_ref, x_ref, next_ref, cw_ref, cb_ref, wg_ref, bg_ref, lam_ref,
         o_ref, ext_ref, a_ref, b_ref, carry_ref) = refs
    n = pl.program_id(1)
    tile = _lru_tile(n, rev, s_len, r_len, n_tiles)
    tm = x_ref.shape[1]
    bw = wg_ref.shape[1]

    @pl.when(n == 0)
    def _():
        carry_ref[...] = jnp.zeros(carry_ref.shape, F32)

    xc = _conv_taps(ext_ref, prev_ref, x_ref, next_ref, cw_ref, tile, s_len, r_len) + cb_ref[...]
    res = [_dot(xc[:, kb * bw:(kb + 1) * bw], wg_ref[kb]) for kb in range(LRU_BLOCKS)]
    r_gate = jax.nn.sigmoid(jnp.concatenate([t[:, :bw] for t in res], axis=1) + bg_ref[0:1])
    i_gate = jax.nn.sigmoid(jnp.concatenate([t[:, bw:] for t in res], axis=1) + bg_ref[1:2])
    log_a = -LRU_C * r_gate * jax.nn.softplus(-lam_ref[...])
    a_ref[...] = jnp.exp(log_a)
    b_ref[...] = jnp.sqrt(-_expm1(2.0 * log_a)) * (i_gate * xc)

    row = lax.broadcasted_iota(jnp.int32, (8, a_ref.shape[1]), 0)
    groups = tm // 8

    def group(gi, carry):
        g0 = pl.multiple_of((groups - 1 - gi if rev else gi) * 8, 8)
        acc_a = a_ref[pl.ds(g0, 8), :]
        acc_b = b_ref[pl.ds(g0, 8), :]
        for sh in (1, 2, 4):
            keep = (row < 8 - sh) if rev else (row >= sh)
            amt = 8 - sh if rev else sh
            sh_b = pltpu.roll(acc_b, amt, 0)
            sh_a = pltpu.roll(acc_a, amt, 0)
            acc_b = jnp.where(keep, acc_a * sh_b + acc_b, acc_b)
            acc_a = jnp.where(keep, acc_a * sh_a, acc_a)
        hs = acc_a * carry + acc_b
        if fuse_mix:
            o_ref[0, pl.ds(g0, 8), :] = (gl_ref[0, pl.ds(g0, 8), :] * (hf_ref[0, pl.ds(g0, 8), :] + hs)).astype(BF16)
        else:
            o_ref[0, pl.ds(g0, 8), :] = hs
        return hs[0:1] if rev else hs[7:8]

    carry_ref[...] = lax.fori_loop(0, groups, group, carry_ref[...])


def _lru_tile(n, rev, s_len, r_len, n_tiles):
    tm = ROW_TILE
    n_lat = s_len // tm
    n_ctx = n_tiles - n_lat
    if rev:
        return n_tiles - 1 - n
    return jnp.where(n < n_ctx, n_lat + n, n - n_ctx)


def _lru_scan(xr, conv_w, conv_b, w_gate, b_gate, lam_row, s_len, rev, h_fwd=None, gelu=None):
    b, r, width = xr.shape
    tm = ROW_TILE
    n_tiles = r // tm
    fuse_mix = h_fwd is not None
    tile_of = lambda bi, n: _lru_tile(n, rev, s_len, r, n_tiles)
    full = lambda a: pl.BlockSpec(a.shape, lambda bi, n: (0,) * a.ndim)
    cur = pl.BlockSpec((1, tm, width), lambda bi, n: (bi, tile_of(bi, n), 0))
    extra_args, extra_specs = ([h_fwd, gelu], [cur, cur]) if fuse_mix else ([], [])
    return pl.pallas_call(
        functools.partial(_lru_scan_kernel, rev=rev, fuse_mix=fuse_mix, s_len=s_len, r_len=r, n_tiles=n_tiles),
        grid=(b, n_tiles),
        in_specs=_halo_specs(tm, width, tile_of, r)
        + [full(conv_w), full(conv_b), full(w_gate), full(b_gate), full(lam_row)] + extra_specs,
        out_specs=cur,
        out_shape=jax.ShapeDtypeStruct((b, r, width), BF16 if fuse_mix else F32),
        scratch_shapes=[pltpu.VMEM((tm + 16, width), F32), pltpu.VMEM((tm, width), F32),
                        pltpu.VMEM((tm, width), F32), pltpu.VMEM((1, width), F32)],
        compiler_params=_params("parallel", "arbitrary"),
        name="lru_bwd" if rev else "lru_fwd",
    )(xr, xr, xr, conv_w, conv_b, w_gate, b_gate, lam_row, *extra_args)


def _rope_tables(s_len):
    t = jnp.arange(s_len)
    inv = ROPE_THETA ** (-jnp.arange(ROPE_PAIRS, dtype=F32) / ROPE_PAIRS)
    ang_r = (t // GRID_W).astype(F32)[:, None] * inv
    ang_c = (t % GRID_W).astype(F32)[:, None] * inv
    cos = jnp.concatenate([jnp.cos(ang_r)] * 2 + [jnp.cos(ang_c)] * 2, axis=1)
    sin = jnp.concatenate([-jnp.sin(ang_r), jnp.sin(ang_r), -jnp.sin(ang_c), jnp.sin(ang_c)], axis=1)
    return jnp.tile(cos, (1, 2)), jnp.tile(sin, (1, 2))


def kernel(x, c, ctx, c_ctx, ev_norm1, ev_norm2, ev_ada_w, ev_ada_b, ev_w_in, ev_w_out, ev_q_norm, ev_k_norm, ev_lam_q1, ev_lam_k1, ev_lam_q2, ev_lam_k2, ev_sub_norm, ev_gdn_conv, ev_gdn_a_log, ev_gdn_dt_bias, ev_gdn_o_norm, ev_ffn_w_gu, ev_ffn_w_down, od_norm1, od_norm2, od_ada_w, od_ada_b, od_w_in, od_conv_w, od_conv_b, od_w_r, od_b_r, od_w_i, od_b_i, od_lam, od_w_out, od_ffn_w_gu, od_ffn_w_down):
    batch, s_len, d = x.shape
    assert ev_norm1.shape[0] == 1 and od_norm1.shape[0] == 1, "two-layer block: one even and one odd layer"
    assert ctx.shape[1] % ROW_TILE == 0 and s_len % ROW_TILE == 0 and batch + 1 <= 8
    row = lambda v: v.reshape(1, -1).astype(F32)

    h = jnp.concatenate([x, ctx], axis=1)
    cond = jnp.zeros((8, d), F32).at[:batch].set(c).at[batch].set(c_ctx)

    mod0 = _mod_table(_ada_mod(cond, ev_ada_w[0], ev_ada_b[0]), batch, d)
    w_in = ev_w_in[0].astype(BF16)
    qw = DA_HEADS * 2 * DA_QK
    vw = DA_HEADS * DA_V
    gw = GDN_HEADS * GDN_DK
    o_da, o_gdn = 2 * qw + vw, 2 * qw + vw + 3 * gw
    wq, wk, wv = w_in[:, :qw], w_in[:, qw:2 * qw], w_in[:, 2 * qw:o_da]
    wg, wz = w_in[:, o_da:o_gdn], w_in[:, o_gdn:o_gdn + gw]
    wgt = jnp.pad(w_in[:, o_gdn + gw:], ((0, 0), (0, 128 - 4 * GDN_HEADS)))
    grp = jnp.arange(qw) // DA_QK
    gsum = (grp[:, None] == grp[None, :]).astype(BF16)
    qgain = jnp.tile(ev_q_norm[0], qw // DA_QK).reshape(1, qw)
    kgain = jnp.tile(ev_k_norm[0], qw // DA_QK).reshape(1, qw)
    cos, sin = _rope_tables(s_len)
    qz, k, vt, ug, z, gt = _inproj0(h, mod0, row(ev_norm1[0]), wq, wk, wv, wg, wz, wgt, gsum, qgain, kgain,
                                    cos, sin, s_len)
    lambda_init = 0.8 - 0.6 * math.exp(-0.3 * 0)
    lam_rows = jnp.pad(jnp.stack([ev_lam_q1[0], ev_lam_k1[0], ev_lam_q2[0], ev_lam_k2[0]]),
                       ((0, 0), (0, 128 - DA_QK)))
    ax = _diff_attention(qz, k, vt, ev_sub_norm[0].reshape(DA_V, 1), lam_rows, s_len, lambda_init)
    pad16 = lambda v: jnp.zeros((1, 128), F32).at[0, 8:8 + 2 * GDN_HEADS].set(v.reshape(-1))
    prep = _gdn_prep(ug, gt, ev_gdn_conv[0], pad16(ev_gdn_a_log[0]), pad16(ev_gdn_dt_bias[0]), s_len)
    o_f, o_b = _gdn_scan(prep, s_len)
    o_gain = jnp.tile(ev_gdn_o_norm[0], GDN_HEADS).reshape(1, gw)
    h = _post(h, mod0, (ax, o_f, o_b, z, o_gain), row(ev_norm2[0]), ev_w_out[0].astype(BF16),
              ev_ffn_w_gu[0].astype(BF16), ev_ffn_w_down[0].astype(BF16), s_len, h.shape[1])

    mod1 = _mod_table(_ada_mod(cond, od_ada_w[0], od_ada_b[0]), batch, d)
    w_in1 = od_w_in[0].astype(BF16)
    lw = od_conv_b.shape[1]
    gelu, xr = _inproj1(h, mod1, row(od_norm1[0]), w_in1[:, :lw], w_in1[:, lw:], s_len)
    w_gate = jnp.concatenate([od_w_r[0], od_w_i[0]], axis=-1).astype(BF16)
    b_gate = jnp.stack([od_b_r[0], od_b_i[0]], axis=1)
    h_f = _lru_scan(xr, od_conv_w[0], row(od_conv_b[0]), w_gate[0], b_gate[0], row(od_lam[0, 0]), s_len, False)
    mix = _lru_scan(xr, od_conv_w[0], row(od_conv_b[0]), w_gate[1], b_gate[1], row(od_lam[0, 1]), s_len, True,
                    h_fwd=h_f, gelu=gelu)
    return _post(h, mod1, (mix,), row(od_norm2[0]), od_w_out[0].astype(BF16), od_ffn_w_gu[0].astype(BF16),
                 od_ffn_w_down[0].astype(BF16), s_len, s_len)
```

```python
import functools
import math

import jax
import jax.numpy as jnp
from jax import lax
from jax.experimental import pallas as pl
from jax.experimental.pallas import tpu as pltpu

F32 = jnp.float32
BF16 = jnp.bfloat16
EPS = 1e-6
HIGHEST = lax.Precision.HIGHEST

GRID_W = 64
ROPE_THETA = 10000.0
DA_HEADS = 4
DA_QK = 64
DA_V = 128
DA_SCALE = DA_QK ** -0.5
ROPE_PAIRS = DA_QK // 4
GDN_HEADS = 4
GDN_DK = 128
GDN_CHUNK = 64
GDN_STACK = GDN_HEADS * GDN_CHUNK
LRU_BLOCKS = 8
LRU_C = 8.0
LOG2E = math.log2(math.e)

ROW_TILE = 256
VMEM_LIMIT = 56 * 1024 * 1024


def _mxu(a, b):
    return jnp.dot(a, b, preferred_element_type=F32)


def _mxu_nt(a, b):
    return lax.dot_general(a, b, (((1,), (1,)), ((), ())), preferred_element_type=F32)


def _dot(a, b):
    return _mxu(a.astype(BF16), b.astype(BF16))


def _dot_nt(a, b):
    return _mxu_nt(a.astype(BF16), b.astype(BF16))


def _dot_hi(a, b):
    return jnp.dot(a, b, precision=HIGHEST, preferred_element_type=F32)


def _split(a):
    hi = a.astype(BF16)
    return hi, (a - hi.astype(F32)).astype(BF16)


def _dot3(a, b):
    a_hi, a_lo = _split(a)
    b_hi, b_lo = _split(b)
    return _mxu(a_hi, b_hi) + (_mxu(a_hi, b_lo) + _mxu(a_lo, b_hi))


def _silu(x):
    return x * jax.nn.sigmoid(x)


def _modulate(x, gain, shift, scale):
    ms = jnp.mean(x * x, axis=-1, keepdims=True)
    return x * lax.rsqrt(ms + EPS) * gain * (1.0 + scale) + shift


def _params(*sem):
    return pltpu.CompilerParams(dimension_semantics=sem, vmem_limit_bytes=VMEM_LIMIT)


def _ada_kernel(c_ref, w_ref, b_ref, o_ref):
    o_ref[...] = _dot_hi(_silu(c_ref[...]), w_ref[...]) + b_ref[...]


def _ada_mod(cond, w, b):
    d = w.shape[0]
    n = w.shape[1]
    return pl.pallas_call(
        _ada_kernel,
        grid=(n // d,),
        in_specs=[pl.BlockSpec((8, d), lambda j: (0, 0)),
                  pl.BlockSpec((d, d), lambda j: (0, j)),
                  pl.BlockSpec((1, d), lambda j: (0, j))],
        out_specs=pl.BlockSpec((8, d), lambda j: (0, j)),
        out_shape=jax.ShapeDtypeStruct((8, n), F32),
        compiler_params=_params("parallel"),
        name="ada_mod",
    )(cond, w, b.reshape(1, n))


def _mod_table(m, batch, d):
    six = m.reshape(8, 6, d)
    lat = six[:batch]
    ctx = jnp.broadcast_to(six[batch][None], (batch, 6, d))
    t = jnp.stack([lat, ctx], axis=1)
    return jnp.pad(t, ((0, 0), (0, 0), (0, 2), (0, 0)))


def _rope(x, cos, sin_signed):
    n = x.shape[-1]
    lane = lax.broadcasted_iota(jnp.int32, x.shape, 1) % (2 * ROPE_PAIRS)
    partner = jnp.where(lane < ROPE_PAIRS, pltpu.roll(x, n - ROPE_PAIRS, 1), pltpu.roll(x, ROPE_PAIRS, 1))
    return x * cos + partner * sin_signed


def _inproj0_kernel(h_ref, mod_ref, n1_ref, wq_ref, wk_ref, wv_ref, wg_ref, wz_ref, wgt_ref, gsum_ref,
                    qg_ref, kg_ref, cos_ref, sin_ref,
                    qz_ref, k_ref, vt_ref, ug_ref, z_ref, gt_ref, *, n_lat_tiles):
    is_lat = pl.program_id(1) < n_lat_tiles
    y = _modulate(h_ref[0], n1_ref[...], mod_ref[0, 0, 0:1, :], mod_ref[0, 0, 1:2, :])
    yb = y.astype(BF16)
    cos = jnp.concatenate([cos_ref[...]] * DA_HEADS, axis=1)
    sin = jnp.concatenate([sin_ref[...]] * DA_HEADS, axis=1)

    def head_norm_rope(w_ref, gain_ref):
        u = _dot(yb, w_ref[...])
        sq = u * u
        hi = sq.astype(BF16)
        lo = (sq - hi.astype(F32)).astype(BF16)
        ss = _dot(hi, gsum_ref[...]) + _dot(lo, gsum_ref[...])
        un = u * lax.rsqrt(ss * (1.0 / DA_QK) + EPS) * gain_ref[...]
        return jnp.where(is_lat, _rope(un, cos, sin), un)

    q = head_norm_rope(wq_ref, qg_ref) * (DA_SCALE * LOG2E)
    k = head_norm_rope(wk_ref, kg_ref)
    low = lax.broadcasted_iota(jnp.int32, (q.shape[0], DA_V), 1) < DA_QK
    parts = []
    for hd in range(DA_HEADS):
        qh = q[:, hd * DA_V:(hd + 1) * DA_V]
        parts += [jnp.where(low, qh, 0.0), jnp.where(low, 0.0, qh)]
    qz_ref[0] = jnp.concatenate(parts, axis=1).astype(BF16)
    k_ref[0] = k.astype(BF16)
    vt_ref[0] = _dot(yb, wv_ref[...]).T.astype(BF16)
    ug_ref[0] = _dot(yb, wg_ref[...])
    z_ref[0] = _dot(yb, wz_ref[...])
    gt_ref[0] = _dot(yb, wgt_ref[...])


def _inproj0(h, mod, norm1, wq, wk, wv, wg, wz, wgt, gsum, qgain, kgain, cos, sin, s_len):
    b, r, d = h.shape
    tm = ROW_TILE
    n_lat = s_len // tm
    full = lambda a: pl.BlockSpec(a.shape, lambda bi, i: (0,) * a.ndim)
    rows = lambda w: pl.BlockSpec((1, tm, w), lambda bi, i: (bi, i, 0))
    tab = pl.BlockSpec((tm, 128), lambda bi, i: (jnp.minimum(i, n_lat - 1), 0))
    return pl.pallas_call(
        functools.partial(_inproj0_kernel, n_lat_tiles=n_lat),
        grid=(b, r // tm),
        in_specs=[rows(d),
                  pl.BlockSpec((1, 1, 8, d), lambda bi, i: (bi, jnp.where(i < n_lat, 0, 1), 0, 0)),
                  full(norm1), full(wq), full(wk), full(wv), full(wg), full(wz), full(wgt), full(gsum),
                  full(qgain), full(kgain), tab, tab],
        out_specs=[rows(2 * DA_HEADS * DA_V), rows(DA_HEADS * DA_V),
                   pl.BlockSpec((1, DA_HEADS * DA_V, tm), lambda bi, i: (bi, 0, i)),
                   rows(wg.shape[1]), rows(wz.shape[1]), rows(128)],
        out_shape=[jax.ShapeDtypeStruct((b, r, 2 * DA_HEADS * DA_V), BF16),
                   jax.ShapeDtypeStruct((b, r, DA_HEADS * DA_V), BF16),
                   jax.ShapeDtypeStruct((b, DA_HEADS * DA_V, r), BF16),
                   jax.ShapeDtypeStruct((b, r, wg.shape[1]), F32),
                   jax.ShapeDtypeStruct((b, r, wz.shape[1]), F32),
                   jax.ShapeDtypeStruct((b, r, 128), F32)],
        compiler_params=_params("parallel", "parallel"),
        name="inproj0",
    )(h, mod, norm1, wq, wk, wv, wg, wz, wgt, gsum, qgain, kgain, cos, sin)


def _attn_kernel(*refs, tk, n_main, tail, lambda_init, online):
    qz_ref, k_ref, vt_ref, sub_ref, lam_ref = refs[:5]
    o_ref, acc_ref, m_ref, l_ref, s_ref = refs[-5:]
    m_ref[...] = jnp.full(m_ref.shape, -1e30, F32)
    l_ref[...] = jnp.zeros(l_ref.shape, F32)
    acc_ref[...] = jnp.zeros(acc_ref.shape, F32)
    qs = (qz_ref[0, :, 0:DA_V], qz_ref[0, :, DA_V:2 * DA_V])

    def first(kc, mi):
        st = _mxu_nt(kc, qs[mi])
        if online:
            return st
        p = jnp.exp2(st)
        l_ref[mi] = l_ref[mi] + jnp.sum(p, axis=0, keepdims=True)
        return p.astype(BF16)

    def second(x, vc, mi):
        if not online:
            acc_ref[mi] = acc_ref[mi] + _mxu(vc, x)
            return
        m_old = m_ref[mi]
        m_new = jnp.maximum(m_old, jnp.max(x, axis=0, keepdims=True))
        alpha = jnp.exp2(m_old - m_new)
        p = jnp.exp2(x - m_new)
        l_ref[mi] = l_ref[mi] * alpha + jnp.sum(p, axis=0, keepdims=True)
        acc_ref[mi] = acc_ref[mi] * alpha + _mxu(vc, p.astype(BF16))
        m_ref[mi] = m_new

    def qk(j, slot):
        kc = k_ref[0, pl.ds(pl.multiple_of(j * tk, tk), tk), :]
        for mi in range(2):
            s_ref[slot, mi] = first(kc, mi)

    def consume(j, slot):
        vc = vt_ref[0, :, pl.ds(pl.multiple_of(j * tk, tk), tk)]
        for mi in range(2):
            second(s_ref[slot, mi], vc, mi)

    def body(jj, carry):
        j = 2 * jj
        qk(j + 1, 1)
        consume(j, 0)
        qk(j + 2, 0)
        consume(j + 1, 1)
        return carry

    if n_main:
        qk(0, 0)
        lax.fori_loop(0, n_main // 2 - 1, body, 0)
        qk(n_main - 1, 1)
        consume(n_main - 2, 0)
        consume(n_main - 1, 1)
    if tail is not None:
        t0, tn = tail
        kc = k_ref[0, t0:t0 + tn, :]
        vc = vt_ref[0, :, t0:t0 + tn]
        for mi in range(2):
            second(first(kc, mi), vc, mi)

    lam_rows = lam_ref[...]
    e1 = jnp.exp(jnp.sum(lam_rows[0:1] * lam_rows[1:2], axis=-1, keepdims=True))
    e2 = jnp.exp(jnp.sum(lam_rows[2:3] * lam_rows[3:4], axis=-1, keepdims=True))
    lam = e1 - e2 + lambda_init
    o = acc_ref[0] / l_ref[0] - lam * (acc_ref[1] / l_ref[1])
    ms = jnp.mean(o * o, axis=0, keepdims=True)
    y = o * lax.rsqrt(ms + EPS) * sub_ref[...] * (1.0 - lambda_init)
    o_ref[0] = y.T.astype(BF16)


ATTN_TQ = 1024
ATTN_TK = 512


def _attn_call(qz, k, vt, sub_col, lam_rows, lambda_init, *, tq, q_tile0, n_q, key_rows, key_tile0, n_main, tail, name,
               online):
    b = k.shape[0]
    assert n_main % 2 == 0
    kern = functools.partial(_attn_kernel, tk=ATTN_TK, n_main=n_main, tail=tail, lambda_init=lambda_init,
                             online=online)
    return pl.pallas_call(
        kern,
        grid=(b, DA_HEADS, n_q),
        in_specs=[pl.BlockSpec((1, tq, 2 * DA_V), lambda bi, hd, i: (bi, q_tile0 + i, hd)),
                  pl.BlockSpec((1, key_rows, DA_V), lambda bi, hd, i: (bi, key_tile0, hd)),
                  pl.BlockSpec((1, DA_V, key_rows), lambda bi, hd, i: (bi, hd, key_tile0)),
                  pl.BlockSpec((DA_V, 1), lambda bi, hd, i: (0, 0)),
                  pl.BlockSpec((4, 128), lambda bi, hd, i: (0, 0))],
        out_specs=pl.BlockSpec((1, tq, DA_V), lambda bi, hd, i: (bi, i, hd)),
        out_shape=jax.ShapeDtypeStruct((b, n_q * tq, DA_HEADS * DA_V), BF16),
        scratch_shapes=[pltpu.VMEM((2, DA_V, tq), F32), pltpu.VMEM((2, 1, tq), F32),
                        pltpu.VMEM((2, 1, tq), F32),
                        pltpu.VMEM((2, 2, ATTN_TK, tq), F32 if online else BF16)],
        compiler_params=_params("parallel", "parallel", "arbitrary"),
        name=name + ("_online" if online else ""),
    )(qz, k, vt, sub_col, lam_rows)


MAX_UNSHIFTED_LOG2_SCORE = 40.0


def _diff_attention(qz, k, vt, sub_col, lam_rows, q_gain, k_gain, s_len, lambda_init):
    r = k.shape[1]
    c_len = r - s_len
    assert s_len % ATTN_TQ == 0 and s_len % ATTN_TK == 0 and s_len % c_len == 0 and c_len % 128 == 0

    def run(online):
        ax = _attn_call(qz, k, vt, sub_col, lam_rows, lambda_init, tq=ATTN_TQ, q_tile0=0, n_q=s_len // ATTN_TQ,
                        key_rows=r, key_tile0=0, n_main=s_len // ATTN_TK, tail=(s_len, c_len), name="diff_attn",
                        online=online)
        ac = _attn_call(qz, k, vt, sub_col, lam_rows, lambda_init, tq=c_len, q_tile0=s_len // c_len, n_q=1,
                        key_rows=c_len, key_tile0=s_len // c_len, n_main=0, tail=(0, c_len),
                        name="diff_attn_ctx", online=online)
        return jnp.concatenate([ax, ac], axis=1)

    bound = 1.01 * DA_QK * DA_SCALE * LOG2E * jnp.max(jnp.abs(q_gain)) * jnp.max(jnp.abs(k_gain))
    return lax.cond(bound <= MAX_UNSHIFTED_LOG2_SCORE, lambda: run(False), lambda: run(True))


def _conv_taps(ext_ref, prev_ref, x_ref, next_ref, w_ref, tile, s_len, r_len):
    tm = x_ref.shape[1]
    r0 = tile * tm
    has_prev = jnp.logical_and(r0 != 0, r0 != s_len)
    has_next = jnp.logical_and(r0 + tm != s_len, r0 + tm != r_len)
    x = x_ref[0]
    ext_ref[0:8, :] = jnp.where(has_prev, prev_ref[0], 0.0)
    ext_ref[8:8 + tm, :] = x
    ext_ref[8 + tm:16 + tm, :] = jnp.where(has_next, next_ref[0], 0.0)
    w = w_ref[...]
    return (ext_ref[pl.ds(6, tm), :] * w[0:1] + ext_ref[pl.ds(7, tm), :] * w[1:2]
            + x * w[2:3] + ext_ref[pl.ds(9, tm), :] * w[3:4])


def _halo_specs(tm, width, tile_of, n_rows):
    blocks = n_rows // 8
    cur = lambda *g: tile_of(*g)
    return [pl.BlockSpec((1, 8, width), lambda *g: (g[0], jnp.maximum(cur(*g) * (tm // 8) - 1, 0), 0)),
            pl.BlockSpec((1, tm, width), lambda *g: (g[0], cur(*g), 0)),
            pl.BlockSpec((1, 8, width), lambda *g: (g[0], jnp.minimum((cur(*g) + 1) * (tm // 8), blocks - 1), 0))]


def _col(x, lane):
    return x[:, lane:lane + 1]


def _gdn_prep_kernel(prev_ref, x_ref, next_ref, gt_ref, cw_ref, arow_ref, dtrow_ref,
                     w_ref, u_ref, qg_ref, kdt_ref, att_ref, gl_ref, ext_ref, *, s_len, r_len):
    tile = pl.program_id(1)
    tm = x_ref.shape[1]
    hw = GDN_HEADS * GDN_DK
    qkv = _silu(_conv_taps(ext_ref, prev_ref, x_ref, next_ref, cw_ref, tile, s_len, r_len))

    def l2n(a):
        return a * lax.rsqrt(jnp.sum(a * a, axis=-1, keepdims=True) + EPS)

    qh = [l2n(qkv[:, hd * GDN_DK:(hd + 1) * GDN_DK]) * GDN_DK ** -0.5 for hd in range(GDN_HEADS)]
    kh = [l2n(qkv[:, hw + hd * GDN_DK:hw + (hd + 1) * GDN_DK]) for hd in range(GDN_HEADS)]
    vh = [qkv[:, 2 * hw + hd * GDN_DK:2 * hw + (hd + 1) * GDN_DK] for hd in range(GDN_HEADS)]

    gt = gt_ref[0]
    beta_all = jax.nn.sigmoid(gt)
    g_all = -jnp.exp(arow_ref[...]) * jax.nn.softplus(gt + dtrow_ref[...])

    c = GDN_CHUNK
    st = GDN_STACK
    rr = lax.broadcasted_iota(jnp.int32, (st, st), 0)
    cc = lax.broadcasted_iota(jnp.int32, (st, st), 1)
    same = (rr // c) == (cc // c)
    eye = rr == cc
    row_in_chunk = lax.broadcasted_iota(jnp.int32, (c, 128), 0)

    for ci in range(tm // c):
        sl = slice(ci * c, (ci + 1) * c)
        stack = lambda parts: jnp.concatenate([p[sl] for p in parts], axis=0)
        q_s, k_s, v_s = stack(qh), stack(kh), stack(vh)
        g_c = g_all[sl]
        csum = g_c
        for sh in (1, 2, 4, 8, 16, 32):
            csum = csum + jnp.where(row_in_chunk >= sh, pltpu.roll(csum, sh, 0), 0.0)
        total = csum[c - 1:c]
        suffix = total - csum + g_c
        beta_c = beta_all[sl]
        for di in range(2):
            gcum = csum if di == 0 else suffix
            lanes = [GDN_HEADS * di + hd for hd in range(GDN_HEADS)]
            beta_s = jnp.concatenate([_col(beta_c, ln) for ln in lanes], axis=0)
            gc_s = jnp.concatenate([_col(gcum, 8 + ln) for ln in lanes], axis=0)
            gl_s = jnp.concatenate([jnp.broadcast_to(_col(total, 8 + ln), (c, 1)) for ln in lanes], axis=0)
            gc_row = jnp.sum(jnp.where(eye, jnp.broadcast_to(gc_s, (st, st)), 0.0), axis=0, keepdims=True)
            tri = (rr >= cc) if di == 0 else (rr <= cc)
            incl = jnp.logical_and(same, tri)
            strict = jnp.logical_and(incl, jnp.logical_not(eye))
            dec = jnp.where(incl, jnp.exp(jnp.minimum(gc_s - gc_row, 0.0)), 0.0)
            kb_s = k_s * beta_s
            a_mat = jnp.where(strict, _dot_nt(kb_s, k_s) * dec, 0.0)
            att = _dot_nt(q_s, k_s) * dec
            ident = jnp.where(eye, 1.0, 0.0)
            pw = -a_mat
            t_inv = ident + pw
            for _ in range(5):
                pw = _dot(pw, pw)
                t_inv = t_inv + _dot(t_inv, pw)
            resid = ident - t_inv - _dot3(a_mat, t_inv)
            t_inv = t_inv + _dot(t_inv, resid)
            sol = _dot3(t_inv, jnp.concatenate([v_s * beta_s, kb_s * jnp.exp(gc_s)], axis=1))
            u_ref[0, di, ci] = sol[:, :GDN_DK]
            w_ref[0, di, ci] = sol[:, GDN_DK:].astype(BF16)
            qg_ref[0, di, ci] = (q_s * jnp.exp(gc_s)).astype(BF16)
            kdt_ref[0, di, ci] = (k_s * jnp.exp(gl_s - gc_s)).T.astype(BF16)
            att_ref[0, di, ci] = att.astype(BF16)
            gl_row = jnp.concatenate(
                [jnp.broadcast_to(jnp.exp(_col(total, 8 + ln)), (8, GDN_DK)) for ln in lanes], axis=1)
            gl_ref[0, di, ci] = gl_row


def _gdn_prep(ug, gt, conv_w, arow, dtrow, s_len):
    b, r, width = ug.shape
    tm = ROW_TILE
    cpt = tm // GDN_CHUNK
    nc = r // GDN_CHUNK
    st = GDN_STACK
    full = lambda a: pl.BlockSpec(a.shape, lambda bi, i: (0,) * a.ndim)
    out = lambda *tail: pl.BlockSpec((1, 2, cpt) + tail, lambda bi, i: (bi, 0, i) + (0,) * len(tail))
    shp = lambda dt, *tail: jax.ShapeDtypeStruct((b, 2, nc) + tail, dt)
    return pl.pallas_call(
        functools.partial(_gdn_prep_kernel, s_len=s_len, r_len=r),
        grid=(b, r // tm),
        in_specs=_halo_specs(tm, width, lambda bi, i: i, r)
        + [pl.BlockSpec((1, tm, 128), lambda bi, i: (bi, i, 0)), full(conv_w), full(arow), full(dtrow)],
        out_specs=[out(st, GDN_DK), out(st, GDN_DK), out(st, GDN_DK), out(GDN_DK, st), out(st, st),
                   out(8, GDN_HEADS * GDN_DK)],
        out_shape=[shp(BF16, st, GDN_DK), shp(F32, st, GDN_DK), shp(BF16, st, GDN_DK), shp(BF16, GDN_DK, st),
                   shp(BF16, st, st), shp(F32, 8, GDN_HEADS * GDN_DK)],
        scratch_shapes=[pltpu.VMEM((tm + 16, width), F32)],
        compiler_params=_params("parallel", "parallel"),
        name="gdn_prep",
    )(ug, ug, ug, gt, conv_w, arow, dtrow)


def _gdn_scan_kernel(*refs):
    ins, o_refs, s_ref = refs[:12], refs[12:14], refs[14]
    c = GDN_CHUNK
    dk = GDN_DK
    st = GDN_STACK

    @pl.when(pl.program_id(0) == 0)
    def _():
        s_ref[...] = jnp.zeros(s_ref.shape, F32)

    rr = lax.broadcasted_iota(jnp.int32, (st, GDN_HEADS * dk), 0) // c
    cc = lax.broadcasted_iota(jnp.int32, (st, GDN_HEADS * dk), 1) // dk
    block = rr == cc
    diag = lambda m: jnp.concatenate([m[hd * c:(hd + 1) * c, hd * dk:(hd + 1) * dk] for hd in range(GDN_HEADS)],
                                     axis=0)
    for bi in range(s_ref.shape[0]):
        for di in range(2):
            w_ref, u_ref, qg_ref, kdt_ref, att_ref, gl_ref = ins[6 * di:6 * di + 6]
            s_all = s_ref[bi, di]
            r1 = _dot(jnp.concatenate([w_ref[bi, 0, 0], qg_ref[bi, 0, 0]], axis=0), s_all)
            v_new = u_ref[bi, 0, 0] - diag(r1[:st])
            v_bd = jnp.where(block, jnp.concatenate([v_new] * GDN_HEADS, axis=1), 0.0)
            r2 = _dot(jnp.concatenate([att_ref[bi, 0, 0], kdt_ref[bi, 0, 0]], axis=0), v_bd)
            o_s = diag(r1[st:]) + diag(r2[:st])
            o_refs[di][bi] = jnp.concatenate([o_s[hd * c:(hd + 1) * c] for hd in range(GDN_HEADS)], axis=1)
            s_ref[bi, di] = s_all * gl_ref[bi, 0, 0, 0:1, :] + r2[st:]


def _gdn_scan(prep, s_len):
    w, u, qg, kdt, att, gl = prep
    b, _, nc = w.shape[:3]
    nc_lat = s_len // GDN_CHUNK
    nc_ctx = nc - nc_lat

    def chunk(di, n):
        if di == 0:
            return jnp.where(n < nc_ctx, nc_lat + n, n - nc_ctx)
        return nc - 1 - n

    def spec(a, di):
        tail = a.shape[3:]
        return pl.BlockSpec((b, 1, 1) + tail, lambda n: (0, di, chunk(di, n)) + (0,) * len(tail))

    hv = GDN_HEADS * GDN_DK
    r = nc * GDN_CHUNK
    return pl.pallas_call(
        _gdn_scan_kernel,
        grid=(nc,),
        in_specs=[spec(a, di) for di in range(2) for a in (w, u, qg, kdt, att, gl)],
        out_specs=[pl.BlockSpec((b, GDN_CHUNK, hv), lambda n, di=di: (0, chunk(di, n), 0)) for di in range(2)],
        out_shape=[jax.ShapeDtypeStruct((b, r, hv), F32)] * 2,
        scratch_shapes=[pltpu.VMEM((b, 2, GDN_DK, hv), F32)],
        compiler_params=_params("arbitrary"),
        name="gdn_scan",
    )(*([w, u, qg, kdt, att, gl] * 2))


def _post_kernel(*refs, gdn_mix, hidden, th):
    if gdn_mix:
        h_ref, mod_ref, ax_ref, of_ref, ob_ref, z_ref, on_ref, n2_ref, wo_ref, wgu_ref, wd_ref, o_ref = refs
    else:
        h_ref, mod_ref, mix_ref, n2_ref, wo_ref, wgu_ref, wd_ref, o_ref = refs
    mod = mod_ref[0, 0]
    gate1, shift2, scale2, gate2 = mod[2:3], mod[3:4], mod[4:5], mod[5:6]
    if gdn_mix:
        o = of_ref[0] + ob_ref[0]
        parts = []
        for hd in range(GDN_HEADS):
            oh = o[:, hd * GDN_DK:(hd + 1) * GDN_DK]
            parts.append(oh * lax.rsqrt(jnp.mean(oh * oh, axis=-1, keepdims=True) + EPS))
        bx = jnp.concatenate(parts, axis=1) * on_ref[...] * _silu(z_ref[0])
        half = ax_ref.shape[2]
        mixed = _dot(ax_ref[0], wo_ref[0:half, :]) + _dot(bx, wo_ref[half:, :])
    else:
        mixed = _dot(mix_ref[0], wo_ref[...])
    h1 = h_ref[0] + gate1 * mixed
    y = _modulate(h1, n2_ref[...], shift2, scale2).astype(BF16)
    acc = jnp.zeros(h1.shape, F32)
    for j in range(hidden // th):
        g = _dot(y, wgu_ref[:, j * th:(j + 1) * th])
        u = _dot(y, wgu_ref[:, hidden + j * th:hidden + (j + 1) * th])
        acc = acc + _dot(_silu(g) * u, wd_ref[j * th:(j + 1) * th, :])
    o_ref[0] = h1 + gate2 * acc


def _post(h, mod, mix_inputs, norm2, w_out, w_gu, w_down, s_len, rows_out):
    b, _, d = h.shape
    tm = ROW_TILE
    n_lat = s_len // tm
    hidden = w_down.shape[0]
    gdn_mix = len(mix_inputs) > 1
    full = lambda a: pl.BlockSpec(a.shape, lambda bi, i: (0,) * a.ndim)
    rows = lambda w: pl.BlockSpec((1, tm, w), lambda bi, i: (bi, i, 0))
    if gdn_mix:
        ax, o_f, o_b, z, o_gain = mix_inputs
        mix_args = [ax, o_f, o_b, z, o_gain]
        mix_specs = [rows(ax.shape[2]), rows(o_f.shape[2]), rows(o_b.shape[2]), rows(z.shape[2]), full(o_gain)]
    else:
        mix_args = list(mix_inputs)
        mix_specs = [rows(mix_inputs[0].shape[2])]
    return pl.pallas_call(
        functools.partial(_post_kernel, gdn_mix=gdn_mix, hidden=hidden, th=256),
        grid=(b, rows_out // tm),
        in_specs=[rows(d), pl.BlockSpec((1, 1, 8, d), lambda bi, i: (bi, jnp.where(i < n_lat, 0, 1), 0, 0))]
        + mix_specs + [full(norm2), full(w_out), full(w_gu), full(w_down)],
        out_specs=rows(d),
        out_shape=jax.ShapeDtypeStruct((b, rows_out, d), F32),
        compiler_params=_params("parallel", "parallel"),
        name="post_gdn" if gdn_mix else "post_lru",
    )(h, mod, *mix_args, norm2, w_out, w_gu, w_down)


def _inproj1_kernel(h_ref, mod_ref, n1_ref, wa_ref, wb_ref, gl_ref, xr_ref):
    y = _modulate(h_ref[0], n1_ref[...], mod_ref[0, 0, 0:1, :], mod_ref[0, 0, 1:2, :]).astype(BF16)
    gl_ref[0] = jax.nn.gelu(_dot(y, wa_ref[...]))
    xr_ref[0] = _dot(y, wb_ref[...])


def _inproj1(h, mod, norm1, wa, wb, s_len):
    b, r, d = h.shape
    tm = ROW_TILE
    n_lat = s_len // tm
    full = lambda a: pl.BlockSpec(a.shape, lambda bi, i: (0,) * a.ndim)
    rows = lambda w: pl.BlockSpec((1, tm, w), lambda bi, i: (bi, i, 0))
    return pl.pallas_call(
        _inproj1_kernel,
        grid=(b, r // tm),
        in_specs=[rows(d), pl.BlockSpec((1, 1, 8, d), lambda bi, i: (bi, jnp.where(i < n_lat, 0, 1), 0, 0)),
                  full(norm1), full(wa), full(wb)],
        out_specs=[rows(wa.shape[1]), rows(wb.shape[1])],
        out_shape=[jax.ShapeDtypeStruct((b, r, wa.shape[1]), F32), jax.ShapeDtypeStruct((b, r, wb.shape[1]), F32)],
        compiler_params=_params("parallel", "parallel"),
        name="inproj1",
    )(h, mod, norm1, wa, wb)


def _lru_scan_kernel(*refs, rev, fuse_mix, s_len, r_len, n_tiles):
    if fuse_mix:
        (prev_ref, x_ref, next_ref, cw_ref, cb_ref, wg_ref, bg_ref, lam_ref, hf_ref, gl_ref,
         o_ref, ext_ref, a_ref, b_ref, carry_ref) = refs
    else:
        (prev_ref, x_ref, next_ref, cw_ref, cb_ref, wg_ref, bg_ref, lam_ref,
         o_ref, ext_ref, a_ref, b_ref, carry_ref) = refs
    n = pl.program_id(1)
    tile = _lru_tile(n, rev, s_len, r_len, n_tiles)
    tm = x_ref.shape[1]
    bw = wg_ref.shape[1]

    @pl.when(n == 0)
    def _():
        carry_ref[...] = jnp.zeros(carry_ref.shape, F32)

    xc = _conv_taps(ext_ref, prev_ref, x_ref, next_ref, cw_ref, tile, s_len, r_len) + cb_ref[...]
    res = [_dot(xc[:, kb * bw:(kb + 1) * bw], wg_ref[kb]) for kb in range(LRU_BLOCKS)]
    r_gate = jax.nn.sigmoid(jnp.concatenate([t[:, :bw] for t in res], axis=1) + bg_ref[0:1])
    i_gate = jax.nn.sigmoid(jnp.concatenate([t[:, bw:] for t in res], axis=1) + bg_ref[1:2])
    log_a = -LRU_C * r_gate * jax.nn.softplus(-lam_ref[...])
    a = jnp.exp(log_a)
    a_ref[...] = a
    b_ref[...] = jnp.sqrt(-jnp.tanh(log_a) * (a * a + 1.0)) * (i_gate * xc)

    row = lax.broadcasted_iota(jnp.int32, (8, a_ref.shape[1]), 0)
    groups = tm // 8

    def group(gi, carry):
        g0 = pl.multiple_of((groups - 1 - gi if rev else gi) * 8, 8)
        acc_a = a_ref[pl.ds(g0, 8), :]
        acc_b = b_ref[pl.ds(g0, 8), :]
        for sh in (1, 2, 4):
            keep = (row < 8 - sh) if rev else (row >= sh)
            amt = 8 - sh if rev else sh
            sh_b = pltpu.roll(acc_b, amt, 0)
            sh_a = pltpu.roll(acc_a, amt, 0)
            acc_b = jnp.where(keep, acc_a * sh_b + acc_b, acc_b)
            acc_a = jnp.where(keep, acc_a * sh_a, acc_a)
        hs = acc_a * carry + acc_b
        if fuse_mix:
            o_ref[0, pl.ds(g0, 8), :] = (gl_ref[0, pl.ds(g0, 8), :] * (hf_ref[0, pl.ds(g0, 8), :] + hs)).astype(BF16)
        else:
            o_ref[0, pl.ds(g0, 8), :] = hs
        return hs[0:1] if rev else hs[7:8]

    carry_ref[...] = lax.fori_loop(0, groups, group, carry_ref[...])


def _lru_tile(n, rev, s_len, r_len, n_tiles):
    tm = ROW_TILE
    n_lat = s_len // tm
    n_ctx = n_tiles - n_lat
    if rev:
        return n_tiles - 1 - n
    return jnp.where(n < n_ctx, n_lat + n, n - n_ctx)


def _lru_scan(xr, conv_w, conv_b, w_gate, b_gate, lam_row, s_len, rev, h_fwd=None, gelu=None):
    b, r, width = xr.shape
    tm = ROW_TILE
    n_tiles = r // tm
    fuse_mix = h_fwd is not None
    tile_of = lambda bi, n: _lru_tile(n, rev, s_len, r, n_tiles)
    full = lambda a: pl.BlockSpec(a.shape, lambda bi, n: (0,) * a.ndim)
    cur = pl.BlockSpec((1, tm, width), lambda bi, n: (bi, tile_of(bi, n), 0))
    extra_args, extra_specs = ([h_fwd, gelu], [cur, cur]) if fuse_mix else ([], [])
    return pl.pallas_call(
        functools.partial(_lru_scan_kernel, rev=rev, fuse_mix=fuse_mix, s_len=s_len, r_len=r, n_tiles=n_tiles),
        grid=(b, n_tiles),
        in_specs=_halo_specs(tm, width, tile_of, r)
        + [full(conv_w), full(conv_b), full(w_gate), full(b_gate), full(lam_row)] + extra_specs,
        out_specs=cur,
        out_shape=jax.ShapeDtypeStruct((b, r, width), BF16 if fuse_mix else F32),
        scratch_shapes=[pltpu.VMEM((tm + 16, width), F32), pltpu.VMEM((tm, width), F32),
                        pltpu.VMEM((tm, width), F32), pltpu.VMEM((1, width), F32)],
        compiler_params=_params("parallel", "arbitrary"),
        name="lru_bwd" if rev else "lru_fwd",
    )(xr, xr, xr, conv_w, conv_b, w_gate, b_gate, lam_row, *extra_args)


def _rope_tables(s_len):
    t = jnp.arange(s_len)
    inv = ROPE_THETA ** (-jnp.arange(ROPE_PAIRS, dtype=F32) / ROPE_PAIRS)
    ang_r = (t // GRID_W).astype(F32)[:, None] * inv
    ang_c = (t % GRID_W).astype(F32)[:, None] * inv
    cos = jnp.concatenate([jnp.cos(ang_r)] * 2 + [jnp.cos(ang_c)] * 2, axis=1)
    sin = jnp.concatenate([-jnp.sin(ang_r), jnp.sin(ang_r), -jnp.sin(ang_c), jnp.sin(ang_c)], axis=1)
    return jnp.tile(cos, (1, 2)), jnp.tile(sin, (1, 2))


def kernel(x, c, ctx, c_ctx, ev_norm1, ev_norm2, ev_ada_w, ev_ada_b, ev_w_in, ev_w_out, ev_q_norm, ev_k_norm, ev_lam_q1, ev_lam_k1, ev_lam_q2, ev_lam_k2, ev_sub_norm, ev_gdn_conv, ev_gdn_a_log, ev_gdn_dt_bias, ev_gdn_o_norm, ev_ffn_w_gu, ev_ffn_w_down, od_norm1, od_norm2, od_ada_w, od_ada_b, od_w_in, od_conv_w, od_conv_b, od_w_r, od_b_r, od_w_i, od_b_i, od_lam, od_w_out, od_ffn_w_gu, od_ffn_w_down):
    batch, s_len, d = x.shape
    assert ev_norm1.shape[0] == 1 and od_norm1.shape[0] == 1, "two-layer block: one even and one odd layer"
    assert ctx.shape[1] % ROW_TILE == 0 and s_len % ROW_TILE == 0 and batch + 1 <= 8
    row = lambda v: v.reshape(1, -1).astype(F32)

    h = jnp.concatenate([x, ctx], axis=1)
    cond = jnp.zeros((8, d), F32).at[:batch].set(c).at[batch].set(c_ctx)

    mod0 = _mod_table(_ada_mod(cond, ev_ada_w[0], ev_ada_b[0]), batch, d)
    w_in = ev_w_in[0].astype(BF16)
    qw = DA_HEADS * 2 * DA_QK
    vw = DA_HEADS * DA_V
    gw = GDN_HEADS * GDN_DK
    o_da, o_gdn = 2 * qw + vw, 2 * qw + vw + 3 * gw
    wq, wk, wv = w_in[:, :qw], w_in[:, qw:2 * qw], w_in[:, 2 * qw:o_da]
    wg, wz = w_in[:, o_da:o_gdn], w_in[:, o_gdn:o_gdn + gw]
    wgt = jnp.pad(w_in[:, o_gdn + gw:], ((0, 0), (0, 128 - 4 * GDN_HEADS)))
    grp = jnp.arange(qw) // DA_QK
    gsum = (grp[:, None] == grp[None, :]).astype(BF16)
    qgain = jnp.tile(ev_q_norm[0], qw // DA_QK).reshape(1, qw)
    kgain = jnp.tile(ev_k_norm[0], qw // DA_QK).reshape(1, qw)
    cos, sin = _rope_tables(s_len)
    qz, k, vt, ug, z, gt = _inproj0(h, mod0, row(ev_norm1[0]), wq, wk, wv, wg, wz, wgt, gsum, qgain, kgain,
                                    cos, sin, s_len)
    lambda_init = 0.8 - 0.6 * math.exp(-0.3 * 0)
    lam_rows = jnp.pad(jnp.stack([ev_lam_q1[0], ev_lam_k1[0], ev_lam_q2[0], ev_lam_k2[0]]),
                       ((0, 0), (0, 128 - DA_QK)))
    ax = _diff_attention(qz, k, vt, ev_sub_norm[0].reshape(DA_V, 1), lam_rows, ev_q_norm[0], ev_k_norm[0], s_len,
                         lambda_init)
    pad16 = lambda v: jnp.zeros((1, 128), F32).at[0, 8:8 + 2 * GDN_HEADS].set(v.reshape(-1))
    prep = _gdn_prep(ug, gt, ev_gdn_conv[0], pad16(ev_gdn_a_log[0]), pad16(ev_gdn_dt_bias[0]), s_len)
    o_f, o_b = _gdn_scan(prep, s_len)
    o_gain = jnp.tile(ev_gdn_o_norm[0], GDN_HEADS).reshape(1, gw)
    h = _post(h, mod0, (ax, o_f, o_b, z, o_gain), row(ev_norm2[0]), ev_w_out[0].astype(BF16),
              ev_ffn_w_gu[0].astype(BF16), ev_ffn_w_down[0].astype(BF16), s_len, h.shape[1])

    mod1 = _mod_table(_ada_mod(cond, od_ada_w[0], od_ada_b[0]), batch, d)
    w_in1 = od_w_in[0].astype(BF16)
    lw = od_conv_b.shape[1]
    gelu, xr = _inproj1(h, mod1, row(od_norm1[0]), w_in1[:, :lw], w_in1[:, lw:], s_len)
    w_gate = jnp.concatenate([od_w_r[0], od_w_i[0]], axis=-1).astype(BF16)
    b_gate = jnp.stack([od_b_r[0], od_b_i[0]], axis=1)
    h_f = _lru_scan(xr, od_conv_w[0], row(od_conv_b[0]), w_gate[0], b_gate[0], row(od_lam[0, 0]), s_len, False)
    mix = _lru_scan(xr, od_conv_w[0], row(od_conv_b[0]), w_gate[1], b_gate[1], row(od_lam[0, 1]), s_len, True,
                    h_fwd=h_f, gelu=gelu)
    return _post(h, mod1, (mix,), row(od_norm2[0]), od_w_out[0].astype(BF16), od_ffn_w_gu[0].astype(BF16),
                 od_ffn_w_down[0].astype(BF16), s_len, s_len)
```

```python
import functools
import math

import jax
import jax.numpy as jnp
from jax import lax
from jax.experimental import pallas as pl
from jax.experimental.pallas import tpu as pltpu

F32 = jnp.float32
BF16 = jnp.bfloat16
EPS = 1e-6
HIGHEST = lax.Precision.HIGHEST

GRID_W = 64
ROPE_THETA = 10000.0
DA_HEADS = 4
DA_QK = 64
DA_V = 128
DA_SCALE = DA_QK ** -0.5
ROPE_PAIRS = DA_QK // 4
GDN_HEADS = 4
GDN_DK = 128
GDN_CHUNK = 64
GDN_STACK = GDN_HEADS * GDN_CHUNK
LRU_BLOCKS = 8
LRU_C = 8.0
LOG2E = math.log2(math.e)

ROW_TILE = 256
VMEM_LIMIT = 56 * 1024 * 1024


def _mxu(a, b):
    return jnp.dot(a, b, preferred_element_type=F32)


def _mxu_nt(a, b):
    return lax.dot_general(a, b, (((1,), (1,)), ((), ())), preferred_element_type=F32)


def _dot(a, b):
    return _mxu(a.astype(BF16), b.astype(BF16))


def _dot_nt(a, b):
    return _mxu_nt(a.astype(BF16), b.astype(BF16))


def _dot_hi(a, b):
    return jnp.dot(a, b, precision=HIGHEST, preferred_element_type=F32)


def _split(a):
    hi = a.astype(BF16)
    return hi, (a - hi.astype(F32)).astype(BF16)


def _dot3(a, b):
    a_hi, a_lo = _split(a)
    b_hi, b_lo = _split(b)
    return _mxu(a_hi, b_hi) + (_mxu(a_hi, b_lo) + _mxu(a_lo, b_hi))


def _sigmoid(x):
    return 0.5 * jnp.tanh(0.5 * x) + 0.5


def _silu(x):
    return x * jax.nn.sigmoid(x)


def _modulate(x, gain, shift, scale):
    ms = jnp.mean(x * x, axis=-1, keepdims=True)
    return x * lax.rsqrt(ms + EPS) * gain * (1.0 + scale) + shift


def _params(*sem):
    return pltpu.CompilerParams(dimension_semantics=sem, vmem_limit_bytes=VMEM_LIMIT)


def _ada_kernel(c_ref, w_ref, b_ref, o_ref):
    o_ref[...] = _dot_hi(_silu(c_ref[...]), w_ref[...]) + b_ref[...]


def _ada_mod(cond, w, b):
    d = w.shape[0]
    n = w.shape[1]
    return pl.pallas_call(
        _ada_kernel,
        grid=(n // d,),
        in_specs=[pl.BlockSpec((8, d), lambda j: (0, 0)),
                  pl.BlockSpec((d, d), lambda j: (0, j)),
                  pl.BlockSpec((1, d), lambda j: (0, j))],
        out_specs=pl.BlockSpec((8, d), lambda j: (0, j)),
        out_shape=jax.ShapeDtypeStruct((8, n), F32),
        compiler_params=_params("parallel"),
        name="ada_mod",
    )(cond, w, b.reshape(1, n))


def _mod_table(m, batch, d):
    six = m.reshape(8, 6, d)
    lat = six[:batch]
    ctx = jnp.broadcast_to(six[batch][None], (batch, 6, d))
    t = jnp.stack([lat, ctx], axis=1)
    return jnp.pad(t, ((0, 0), (0, 0), (0, 2), (0, 0)))


def _rope(x, cos, sin_signed):
    n = x.shape[-1]
    lane = lax.broadcasted_iota(jnp.int32, x.shape, 1) % (2 * ROPE_PAIRS)
    partner = jnp.where(lane < ROPE_PAIRS, pltpu.roll(x, n - ROPE_PAIRS, 1), pltpu.roll(x, ROPE_PAIRS, 1))
    return x * cos + partner * sin_signed


def _inproj0_kernel(h_ref, mod_ref, n1_ref, wq_ref, wk_ref, wv_ref, wg_ref, wz_ref, wgt_ref, gsum_ref,
                    qg_ref, kg_ref, cos_ref, sin_ref,
                    qz_ref, k_ref, vt_ref, ug_ref, z_ref, gt_ref, *, n_lat_tiles):
    is_lat = pl.program_id(1) < n_lat_tiles
    y = _modulate(h_ref[0], n1_ref[...], mod_ref[0, 0, 0:1, :], mod_ref[0, 0, 1:2, :])
    yb = y.astype(BF16)
    cos = jnp.concatenate([cos_ref[...]] * DA_HEADS, axis=1)
    sin = jnp.concatenate([sin_ref[...]] * DA_HEADS, axis=1)

    def head_norm_rope(w_ref, gain_ref):
        u = _dot(yb, w_ref[...])
        sq = u * u
        hi = sq.astype(BF16)
        lo = (sq - hi.astype(F32)).astype(BF16)
        ss = _dot(hi, gsum_ref[...]) + _dot(lo, gsum_ref[...])
        un = u * lax.rsqrt(ss * (1.0 / DA_QK) + EPS) * gain_ref[...]
        return jnp.where(is_lat, _rope(un, cos, sin), un)

    q = head_norm_rope(wq_ref, qg_ref) * (DA_SCALE * LOG2E)
    k = head_norm_rope(wk_ref, kg_ref)
    low = lax.broadcasted_iota(jnp.int32, (q.shape[0], DA_V), 1) < DA_QK
    parts = []
    for hd in range(DA_HEADS):
        qh = q[:, hd * DA_V:(hd + 1) * DA_V]
        parts += [jnp.where(low, qh, 0.0), jnp.where(low, 0.0, qh)]
    qz_ref[0] = jnp.concatenate(parts, axis=1).astype(BF16)
    k_ref[0] = k.astype(BF16)
    vt_ref[0] = _dot(yb, wv_ref[...]).T.astype(BF16)
    ug_ref[0] = _dot(yb, wg_ref[...])
    z_ref[0] = _dot(yb, wz_ref[...])
    gt_ref[0] = _dot(yb, wgt_ref[...])


def _inproj0(h, mod, norm1, wq, wk, wv, wg, wz, wgt, gsum, qgain, kgain, cos, sin, s_len):
    b, r, d = h.shape
    tm = ROW_TILE
    n_lat = s_len // tm
    full = lambda a: pl.BlockSpec(a.shape, lambda bi, i: (0,) * a.ndim)
    rows = lambda w: pl.BlockSpec((1, tm, w), lambda bi, i: (bi, i, 0))
    tab = pl.BlockSpec((tm, 128), lambda bi, i: (jnp.minimum(i, n_lat - 1), 0))
    return pl.pallas_call(
        functools.partial(_inproj0_kernel, n_lat_tiles=n_lat),
        grid=(b, r // tm),
        in_specs=[rows(d),
                  pl.BlockSpec((1, 1, 8, d), lambda bi, i: (bi, jnp.where(i < n_lat, 0, 1), 0, 0)),
                  full(norm1), full(wq), full(wk), full(wv), full(wg), full(wz), full(wgt), full(gsum),
                  full(qgain), full(kgain), tab, tab],
        out_specs=[rows(2 * DA_HEADS * DA_V), rows(DA_HEADS * DA_V),
                   pl.BlockSpec((1, DA_HEADS * DA_V, tm), lambda bi, i: (bi, 0, i)),
                   rows(wg.shape[1]), rows(wz.shape[1]), rows(128)],
        out_shape=[jax.ShapeDtypeStruct((b, r, 2 * DA_HEADS * DA_V), BF16),
                   jax.ShapeDtypeStruct((b, r, DA_HEADS * DA_V), BF16),
                   jax.ShapeDtypeStruct((b, DA_HEADS * DA_V, r), BF16),
                   jax.ShapeDtypeStruct((b, r, wg.shape[1]), F32),
                   jax.ShapeDtypeStruct((b, r, wz.shape[1]), F32),
                   jax.ShapeDtypeStruct((b, r, 128), F32)],
        compiler_params=_params("parallel", "parallel"),
        name="inproj0",
    )(h, mod, norm1, wq, wk, wv, wg, wz, wgt, gsum, qgain, kgain, cos, sin)


def _attn_kernel(*refs, tk, n_main, tail, lambda_init, online):
    qz_ref, k_ref, vt_ref, sub_ref, lam_ref = refs[:5]
    o_ref, acc_ref, m_ref, l_ref, s_ref = refs[-5:]
    m_ref[...] = jnp.full(m_ref.shape, -1e30, F32)
    l_ref[...] = jnp.zeros(l_ref.shape, F32)
    acc_ref[...] = jnp.zeros(acc_ref.shape, F32)
    qs = (qz_ref[0, :, 0:DA_V], qz_ref[0, :, DA_V:2 * DA_V])

    def first(kc, mi):
        st = _mxu_nt(kc, qs[mi])
        if online:
            return st
        p = jnp.exp2(st)
        l_ref[mi] = l_ref[mi] + jnp.sum(p, axis=0, keepdims=True)
        return p.astype(BF16)

    def second(x, vc, mi):
        if not online:
            acc_ref[mi] = acc_ref[mi] + _mxu(vc, x)
            return
        m_old = m_ref[mi]
        m_new = jnp.maximum(m_old, jnp.max(x, axis=0, keepdims=True))
        alpha = jnp.exp2(m_old - m_new)
        p = jnp.exp2(x - m_new)
        l_ref[mi] = l_ref[mi] * alpha + jnp.sum(p, axis=0, keepdims=True)
        acc_ref[mi] = acc_ref[mi] * alpha + _mxu(vc, p.astype(BF16))
        m_ref[mi] = m_new

    def qk(j, slot):
        kc = k_ref[0, pl.ds(pl.multiple_of(j * tk, tk), tk), :]
        for mi in range(2):
            s_ref[slot, mi] = first(kc, mi)

    def consume(j, slot):
        vc = vt_ref[0, :, pl.ds(pl.multiple_of(j * tk, tk), tk)]
        for mi in range(2):
            second(s_ref[slot, mi], vc, mi)

    def body(jj, carry):
        j = 2 * jj
        qk(j + 1, 1)
        consume(j, 0)
        qk(j + 2, 0)
        consume(j + 1, 1)
        return carry

    if n_main:
        qk(0, 0)
        lax.fori_loop(0, n_main // 2 - 1, body, 0)
        qk(n_main - 1, 1)
        consume(n_main - 2, 0)
        consume(n_main - 1, 1)
    if tail is not None:
        t0, tn = tail
        kc = k_ref[0, t0:t0 + tn, :]
        vc = vt_ref[0, :, t0:t0 + tn]
        for mi in range(2):
            second(first(kc, mi), vc, mi)

    lam_rows = lam_ref[...]
    e1 = jnp.exp(jnp.sum(lam_rows[0:1] * lam_rows[1:2], axis=-1, keepdims=True))
    e2 = jnp.exp(jnp.sum(lam_rows[2:3] * lam_rows[3:4], axis=-1, keepdims=True))
    lam = e1 - e2 + lambda_init
    o = acc_ref[0] / l_ref[0] - lam * (acc_ref[1] / l_ref[1])
    ms = jnp.mean(o * o, axis=0, keepdims=True)
    y = o * lax.rsqrt(ms + EPS) * sub_ref[...] * (1.0 - lambda_init)
    o_ref[0] = y.T.astype(BF16)


ATTN_TQ = 2048
ATTN_TK = 512


def _attn_call(qz, k, vt, sub_col, lam_rows, lambda_init, *, tq, q_tile0, n_q, key_rows, key_tile0, n_main, tail, name,
               online):
    b = k.shape[0]
    assert n_main % 2 == 0
    kern = functools.partial(_attn_kernel, tk=ATTN_TK, n_main=n_main, tail=tail, lambda_init=lambda_init,
                             online=online)
    return pl.pallas_call(
        kern,
        grid=(b, DA_HEADS, n_q),
        in_specs=[pl.BlockSpec((1, tq, 2 * DA_V), lambda bi, hd, i: (bi, q_tile0 + i, hd)),
                  pl.BlockSpec((1, key_rows, DA_V), lambda bi, hd, i: (bi, key_tile0, hd)),
                  pl.BlockSpec((1, DA_V, key_rows), lambda bi, hd, i: (bi, hd, key_tile0)),
                  pl.BlockSpec((DA_V, 1), lambda bi, hd, i: (0, 0)),
                  pl.BlockSpec((4, 128), lambda bi, hd, i: (0, 0))],
        out_specs=pl.BlockSpec((1, tq, DA_V), lambda bi, hd, i: (bi, i, hd)),
        out_shape=jax.ShapeDtypeStruct((b, n_q * tq, DA_HEADS * DA_V), BF16),
        scratch_shapes=[pltpu.VMEM((2, DA_V, tq), F32), pltpu.VMEM((2, 1, tq), F32),
                        pltpu.VMEM((2, 1, tq), F32),
                        pltpu.VMEM((2, 2, ATTN_TK, tq), F32 if online else BF16)],
        compiler_params=_params("parallel", "parallel", "arbitrary"),
        name=name + ("_online" if online else ""),
    )(qz, k, vt, sub_col, lam_rows)


MAX_UNSHIFTED_LOG2_SCORE = 40.0


def _diff_attention(qz, k, vt, sub_col, lam_rows, q_gain, k_gain, s_len, lambda_init):
    r = k.shape[1]
    c_len = r - s_len
    assert s_len % ATTN_TQ == 0 and s_len % ATTN_TK == 0 and s_len % c_len == 0 and c_len % 128 == 0

    def run(online):
        ax = _attn_call(qz, k, vt, sub_col, lam_rows, lambda_init, tq=ATTN_TQ, q_tile0=0, n_q=s_len // ATTN_TQ,
                        key_rows=r, key_tile0=0, n_main=s_len // ATTN_TK, tail=(s_len, c_len), name="diff_attn",
                        online=online)
        ac = _attn_call(qz, k, vt, sub_col, lam_rows, lambda_init, tq=c_len, q_tile0=s_len // c_len, n_q=1,
                        key_rows=c_len, key_tile0=s_len // c_len, n_main=0, tail=(0, c_len),
                        name="diff_attn_ctx", online=online)
        return jnp.concatenate([ax, ac], axis=1)

    bound = 1.01 * DA_QK * DA_SCALE * LOG2E * jnp.max(jnp.abs(q_gain)) * jnp.max(jnp.abs(k_gain))
    return lax.cond(bound <= MAX_UNSHIFTED_LOG2_SCORE, lambda: run(False), lambda: run(True))


def _conv_taps(ext_ref, prev_ref, x_ref, next_ref, w_ref, tile, s_len, r_len):
    tm = x_ref.shape[1]
    r0 = tile * tm
    has_prev = jnp.logical_and(r0 != 0, r0 != s_len)
    has_next = jnp.logical_and(r0 + tm != s_len, r0 + tm != r_len)
    x = x_ref[0]
    ext_ref[0:8, :] = jnp.where(has_prev, prev_ref[0], 0.0)
    ext_ref[8:8 + tm, :] = x
    ext_ref[8 + tm:16 + tm, :] = jnp.where(has_next, next_ref[0], 0.0)
    w = w_ref[...]
    return (ext_ref[pl.ds(6, tm), :] * w[0:1] + ext_ref[pl.ds(7, tm), :] * w[1:2]
            + x * w[2:3] + ext_ref[pl.ds(9, tm), :] * w[3:4])


def _halo_specs(tm, width, tile_of, n_rows):
    blocks = n_rows // 8
    cur = lambda *g: tile_of(*g)
    return [pl.BlockSpec((1, 8, width), lambda *g: (g[0], jnp.maximum(cur(*g) * (tm // 8) - 1, 0), 0)),
            pl.BlockSpec((1, tm, width), lambda *g: (g[0], cur(*g), 0)),
            pl.BlockSpec((1, 8, width), lambda *g: (g[0], jnp.minimum((cur(*g) + 1) * (tm // 8), blocks - 1), 0))]


def _col(x, lane):
    return x[:, lane:lane + 1]


def _gdn_prep_kernel(prev_ref, x_ref, next_ref, gt_ref, cw_ref, arow_ref, dtrow_ref,
                     w_ref, u_ref, qg_ref, kdt_ref, att_ref, gl_ref, ext_ref, *, s_len, r_len):
    tile = pl.program_id(1)
    tm = x_ref.shape[1]
    hw = GDN_HEADS * GDN_DK
    qkv = _silu(_conv_taps(ext_ref, prev_ref, x_ref, next_ref, cw_ref, tile, s_len, r_len))

    def l2n(a):
        return a * lax.rsqrt(jnp.sum(a * a, axis=-1, keepdims=True) + EPS)

    qh = [l2n(qkv[:, hd * GDN_DK:(hd + 1) * GDN_DK]) * GDN_DK ** -0.5 for hd in range(GDN_HEADS)]
    kh = [l2n(qkv[:, hw + hd * GDN_DK:hw + (hd + 1) * GDN_DK]) for hd in range(GDN_HEADS)]
    vh = [qkv[:, 2 * hw + hd * GDN_DK:2 * hw + (hd + 1) * GDN_DK] for hd in range(GDN_HEADS)]

    gt = gt_ref[0]
    beta_all = jax.nn.sigmoid(gt)
    g_all = -jnp.exp(arow_ref[...]) * jax.nn.softplus(gt + dtrow_ref[...])

    c = GDN_CHUNK
    st = GDN_STACK
    rr = lax.broadcasted_iota(jnp.int32, (st, st), 0)
    cc = lax.broadcasted_iota(jnp.int32, (st, st), 1)
    same = (rr // c) == (cc // c)
    eye = rr == cc
    ident = jnp.where(eye, 1.0, 0.0)
    masks = []
    for tri in (rr >= cc, rr <= cc):
        incl = jnp.logical_and(same, tri)
        masks.append((incl, jnp.logical_and(incl, jnp.logical_not(eye))))
    joins = ([], [])
    m = 1
    while m < c:
        pair = (rr // (2 * m)) == (cc // (2 * m))
        r_hi, c_hi = (rr // m) % 2 == 1, (cc // m) % 2 == 1
        joins[0].append(jnp.logical_and(pair, jnp.logical_and(r_hi, jnp.logical_not(c_hi))))
        joins[1].append(jnp.logical_and(pair, jnp.logical_and(c_hi, jnp.logical_not(r_hi))))
        m *= 2
    row_in_chunk = lax.broadcasted_iota(jnp.int32, (c, 128), 0)
    chains = []

    for ci in range(tm // c):
        sl = slice(ci * c, (ci + 1) * c)
        stack = lambda parts: jnp.concatenate([p[sl] for p in parts], axis=0)
        q_s, k_s, v_s = stack(qh), stack(kh), stack(vh)
        g_c = g_all[sl]
        csum = g_c
        for sh in (1, 2, 4, 8, 16, 32):
            csum = csum + jnp.where(row_in_chunk >= sh, pltpu.roll(csum, sh, 0), 0.0)
        total = csum[c - 1:c]
        suffix = total - csum + g_c
        beta_c = beta_all[sl]
        for di in range(2):
            gcum = csum if di == 0 else suffix
            lanes = [GDN_HEADS * di + hd for hd in range(GDN_HEADS)]
            beta_s = jnp.concatenate([_col(beta_c, ln) for ln in lanes], axis=0)
            gc_s = jnp.concatenate([_col(gcum, 8 + ln) for ln in lanes], axis=0)
            gl_s = jnp.concatenate([jnp.broadcast_to(_col(total, 8 + ln), (c, 1)) for ln in lanes], axis=0)
            gc_row = jnp.sum(jnp.where(eye, jnp.broadcast_to(gc_s, (st, st)), 0.0), axis=0, keepdims=True)
            incl, strict = masks[di]
            dec = jnp.where(incl, jnp.exp(jnp.minimum(gc_s - gc_row, 0.0)), 0.0)
            kb_s = k_s * beta_s
            a_mat = jnp.where(strict, _dot_nt(kb_s, k_s) * dec, 0.0)
            qg_ref[0, di, ci] = (q_s * jnp.exp(gc_s)).astype(BF16)
            kdt_ref[0, di, ci] = (k_s * jnp.exp(gl_s - gc_s)).T.astype(BF16)
            att_ref[0, di, ci] = (_dot_nt(q_s, k_s) * dec).astype(BF16)
            gl_ref[0, di, ci] = jnp.concatenate(
                [jnp.broadcast_to(jnp.exp(_col(total, 8 + ln)), (8, GDN_DK)) for ln in lanes], axis=1)
            chains.append((di, ci, a_mat, jnp.concatenate([v_s * beta_s, kb_s * jnp.exp(gc_s)], axis=1)))

    t_invs = [ident - jnp.where(joins[di][0], a_mat, 0.0) for di, _, a_mat, _ in chains]
    for lvl in range(1, len(joins[0])):
        t_invs = [t - _dot(_dot(t, jnp.where(joins[di][lvl], a_mat, 0.0)), t)
                  for t, (di, _, a_mat, _) in zip(t_invs, chains)]
    resids = [ident - t - _dot3(a_mat, t) for t, (_, _, a_mat, _) in zip(t_invs, chains)]
    t_invs = [t + _dot(t, rs) for t, rs in zip(t_invs, resids)]
    for t, (di, ci, _, rhs) in zip(t_invs, chains):
        sol = _dot3(t, rhs)
        u_ref[0, di, ci] = sol[:, :GDN_DK]
        w_ref[0, di, ci] = sol[:, GDN_DK:].astype(BF16)


def _gdn_prep(ug, gt, conv_w, arow, dtrow, s_len):
    b, r, width = ug.shape
    tm = ROW_TILE
    cpt = tm // GDN_CHUNK
    nc = r // GDN_CHUNK
    st = GDN_STACK
    full = lambda a: pl.BlockSpec(a.shape, lambda bi, i: (0,) * a.ndim)
    out = lambda *tail: pl.BlockSpec((1, 2, cpt) + tail, lambda bi, i: (bi, 0, i) + (0,) * len(tail))
    shp = lambda dt, *tail: jax.ShapeDtypeStruct((b, 2, nc) + tail, dt)
    return pl.pallas_call(
        functools.partial(_gdn_prep_kernel, s_len=s_len, r_len=r),
        grid=(b, r // tm),
        in_specs=_halo_specs(tm, width, lambda bi, i: i, r)
        + [pl.BlockSpec((1, tm, 128), lambda bi, i: (bi, i, 0)), full(conv_w), full(arow), full(dtrow)],
        out_specs=[out(st, GDN_DK), out(st, GDN_DK), out(st, GDN_DK), out(GDN_DK, st), out(st, st),
                   out(8, GDN_HEADS * GDN_DK)],
        out_shape=[shp(BF16, st, GDN_DK), shp(F32, st, GDN_DK), shp(BF16, st, GDN_DK), shp(BF16, GDN_DK, st),
                   shp(BF16, st, st), shp(F32, 8, GDN_HEADS * GDN_DK)],
        scratch_shapes=[pltpu.VMEM((tm + 16, width), F32)],
        compiler_params=_params("parallel", "parallel"),
        name="gdn_prep",
    )(ug, ug, ug, gt, conv_w, arow, dtrow)


def _gdn_scan_kernel(*refs):
    ins, o_refs, s_ref = refs[:12], refs[12:14], refs[14]
    c = GDN_CHUNK
    dk = GDN_DK
    st = GDN_STACK

    @pl.when(pl.program_id(0) == 0)
    def _():
        s_ref[...] = jnp.zeros(s_ref.shape, F32)

    rr = lax.broadcasted_iota(jnp.int32, (st, GDN_HEADS * dk), 0) // c
    cc = lax.broadcasted_iota(jnp.int32, (st, GDN_HEADS * dk), 1) // dk
    block = rr == cc
    diag = lambda m: jnp.concatenate([m[hd * c:(hd + 1) * c, hd * dk:(hd + 1) * dk] for hd in range(GDN_HEADS)],
                                     axis=0)
    for bi in range(s_ref.shape[0]):
        for di in range(2):
            w_ref, u_ref, qg_ref, kdt_ref, att_ref, gl_ref = ins[6 * di:6 * di + 6]
            s_all = s_ref[bi, di]
            r1 = _dot(jnp.concatenate([w_ref[bi, 0, 0], qg_ref[bi, 0, 0]], axis=0), s_all)
            v_new = u_ref[bi, 0, 0] - diag(r1[:st])
            v_bd = jnp.where(block, jnp.concatenate([v_new] * GDN_HEADS, axis=1), 0.0)
            r2 = _dot(jnp.concatenate([att_ref[bi, 0, 0], kdt_ref[bi, 0, 0]], axis=0), v_bd)
            o_s = diag(r1[st:]) + diag(r2[:st])
            o_refs[di][bi] = jnp.concatenate([o_s[hd * c:(hd + 1) * c] for hd in range(GDN_HEADS)], axis=1)
            s_ref[bi, di] = s_all * gl_ref[bi, 0, 0, 0:1, :] + r2[st:]


def _gdn_scan(prep, s_len):
    w, u, qg, kdt, att, gl = prep
    b, _, nc = w.shape[:3]
    nc_lat = s_len // GDN_CHUNK
    nc_ctx = nc - nc_lat

    def chunk(di, n):
        if di == 0:
            return jnp.where(n < nc_ctx, nc_lat + n, n - nc_ctx)
        return nc - 1 - n

    def spec(a, di):
        tail = a.shape[3:]
        return pl.BlockSpec((b, 1, 1) + tail, lambda n: (0, di, chunk(di, n)) + (0,) * len(tail))

    hv = GDN_HEADS * GDN_DK
    r = nc * GDN_CHUNK
    return pl.pallas_call(
        _gdn_scan_kernel,
        grid=(nc,),
        in_specs=[spec(a, di) for di in range(2) for a in (w, u, qg, kdt, att, gl)],
        out_specs=[pl.BlockSpec((b, GDN_CHUNK, hv), lambda n, di=di: (0, chunk(di, n), 0)) for di in range(2)],
        out_shape=[jax.ShapeDtypeStruct((b, r, hv), F32)] * 2,
        scratch_shapes=[pltpu.VMEM((b, 2, GDN_DK, hv), F32)],
        compiler_params=_params("arbitrary"),
        name="gdn_scan",
    )(*([w, u, qg, kdt, att, gl] * 2))


def _post_kernel(*refs, gdn_mix, hidden, th):
    if gdn_mix:
        h_ref, mod_ref, ax_ref, of_ref, ob_ref, z_ref, on_ref, n2_ref, wo_ref, wgu_ref, wd_ref, o_ref = refs
    else:
        h_ref, mod_ref, mix_ref, n2_ref, wo_ref, wgu_ref, wd_ref, o_ref = refs
    mod = mod_ref[0, 0]
    gate1, shift2, scale2, gate2 = mod[2:3], mod[3:4], mod[4:5], mod[5:6]
    if gdn_mix:
        o = of_ref[0] + ob_ref[0]
        parts = []
        for hd in range(GDN_HEADS):
            oh = o[:, hd * GDN_DK:(hd + 1) * GDN_DK]
            parts.append(oh * lax.rsqrt(jnp.mean(oh * oh, axis=-1, keepdims=True) + EPS))
        bx = jnp.concatenate(parts, axis=1) * on_ref[...] * _silu(z_ref[0])
        half = ax_ref.shape[2]
        mixed = _dot(ax_ref[0], wo_ref[0:half, :]) + _dot(bx, wo_ref[half:, :])
    else:
        mixed = _dot(mix_ref[0], wo_ref[...])
    h1 = h_ref[0] + gate1 * mixed
    y = _modulate(h1, n2_ref[...], shift2, scale2).astype(BF16)
    acc = jnp.zeros(h1.shape, F32)
    for j in range(hidden // th):
        g = _dot(y, wgu_ref[:, j * th:(j + 1) * th])
        u = _dot(y, wgu_ref[:, hidden + j * th:hidden + (j + 1) * th])
        acc = acc + _dot(_silu(g) * u, wd_ref[j * th:(j + 1) * th, :])
    o_ref[0] = h1 + gate2 * acc


def _post(h, mod, mix_inputs, norm2, w_out, w_gu, w_down, s_len, rows_out):
    b, _, d = h.shape
    tm = ROW_TILE
    n_lat = s_len // tm
    hidden = w_down.shape[0]
    gdn_mix = len(mix_inputs) > 1
    full = lambda a: pl.BlockSpec(a.shape, lambda bi, i: (0,) * a.ndim)
    rows = lambda w: pl.BlockSpec((1, tm, w), lambda bi, i: (bi, i, 0))
    if gdn_mix:
        ax, o_f, o_b, z, o_gain = mix_inputs
        mix_args = [ax, o_f, o_b, z, o_gain]
        mix_specs = [rows(ax.shape[2]), rows(o_f.shape[2]), rows(o_b.shape[2]), rows(z.shape[2]), full(o_gain)]
    else:
        mix_args = list(mix_inputs)
        mix_specs = [rows(mix_inputs[0].shape[2])]
    return pl.pallas_call(
        functools.partial(_post_kernel, gdn_mix=gdn_mix, hidden=hidden, th=256),
        grid=(b, rows_out // tm),
        in_specs=[rows(d), pl.BlockSpec((1, 1, 8, d), lambda bi, i: (bi, jnp.where(i < n_lat, 0, 1), 0, 0))]
        + mix_specs + [full(norm2), full(w_out), full(w_gu), full(w_down)],
        out_specs=rows(d),
        out_shape=jax.ShapeDtypeStruct((b, rows_out, d), F32),
        compiler_params=_params("parallel", "parallel"),
        name="post_gdn" if gdn_mix else "post_lru",
    )(h, mod, *mix_args, norm2, w_out, w_gu, w_down)


def _inproj1_kernel(h_ref, mod_ref, n1_ref, wa_ref, wb_ref, gl_ref, xr_ref):
    y = _modulate(h_ref[0], n1_ref[...], mod_ref[0, 0, 0:1, :], mod_ref[0, 0, 1:2, :]).astype(BF16)
    gl_ref[0] = jax.nn.gelu(_dot(y, wa_ref[...]))
    xr_ref[0] = _dot(y, wb_ref[...])


def _inproj1(h, mod, norm1, wa, wb, s_len):
    b, r, d = h.shape
    tm = ROW_TILE
    n_lat = s_len // tm
    full = lambda a: pl.BlockSpec(a.shape, lambda bi, i: (0,) * a.ndim)
    rows = lambda w: pl.BlockSpec((1, tm, w), lambda bi, i: (bi, i, 0))
    return pl.pallas_call(
        _inproj1_kernel,
        grid=(b, r // tm),
        in_specs=[rows(d), pl.BlockSpec((1, 1, 8, d), lambda bi, i: (bi, jnp.where(i < n_lat, 0, 1), 0, 0)),
                  full(norm1), full(wa), full(wb)],
        out_specs=[rows(wa.shape[1]), rows(wb.shape[1])],
        out_shape=[jax.ShapeDtypeStruct((b, r, wa.shape[1]), F32), jax.ShapeDtypeStruct((b, r, wb.shape[1]), F32)],
        compiler_params=_params("parallel", "parallel"),
        name="inproj1",
    )(h, mod, norm1, wa, wb)


def _lru_scan_kernel(*refs, rev, fuse_mix, s_len, r_len, n_tiles):
    if fuse_mix:
        (prev_ref, x_ref, next_ref, cw_ref, cb_ref, wg_ref, bg_ref, lam_ref, hf_ref, gl_ref,
         o_ref, ext_ref, a_ref, b_ref, carry_ref) = refs
    else:
        (prev_ref, x_ref, next_ref, cw_ref, cb_ref, wg_ref, bg_ref, lam_ref,
         o_ref, ext_ref, a_ref, b_ref, carry_ref) = refs
    n = pl.program_id(1)
    tile = _lru_tile(n, rev, s_len, r_len, n_tiles)
    tm = x_ref.shape[1]
    bw = wg_ref.shape[1]

    @pl.when(n == 0)
    def _():
        carry_ref[...] = jnp.zeros(carry_ref.shape, F32)

    xc = _conv_taps(ext_ref, prev_ref, x_ref, next_ref, cw_ref, tile, s_len, r_len) + cb_ref[...]
    res = [_dot(xc[:, kb * bw:(kb + 1) * bw], wg_ref[kb]) for kb in range(LRU_BLOCKS)]
    r_gate = _sigmoid(jnp.concatenate([t[:, :bw] for t in res], axis=1) + bg_ref[0:1])
    i_gate = _sigmoid(jnp.concatenate([t[:, bw:] for t in res], axis=1) + bg_ref[1:2])
    log_a = -LRU_C * r_gate * jax.nn.softplus(-lam_ref[...])
    a = jnp.exp(log_a)
    a_ref[...] = a
    b_ref[...] = jnp.sqrt(-jnp.tanh(log_a) * (a * a + 1.0)) * (i_gate * xc)

    row = lax.broadcasted_iota(jnp.int32, (8, a_ref.shape[1]), 0)
    groups = tm // 8

    def group(gi, carry):
        g0 = pl.multiple_of((groups - 1 - gi if rev else gi) * 8, 8)
        acc_a = a_ref[pl.ds(g0, 8), :]
        acc_b = b_ref[pl.ds(g0, 8), :]
        for sh in (1, 2, 4):
            keep = (row < 8 - sh) if rev else (row >= sh)
            amt = 8 - sh if rev else sh
            sh_b = pltpu.roll(acc_b, amt, 0)
            sh_a = pltpu.roll(acc_a, amt, 0)
            acc_b = jnp.where(keep, acc_a * sh_b + acc_b, acc_b)
            acc_a = jnp.where(keep, acc_a * sh_a, acc_a)
        hs = acc_a * carry + acc_b
        if fuse_mix:
            o_ref[0, pl.ds(g0, 8), :] = (gl_ref[0, pl.ds(g0, 8), :] * (hf_ref[0, pl.ds(g0, 8), :] + hs)).astype(BF16)
        else:
            o_ref[0, pl.ds(g0, 8), :] = hs
        return hs[0:1] if rev else hs[7:8]

    carry_ref[...] = lax.fori_loop(0, groups, group, carry_ref[...], unroll=4)


def _lru_tile(n, rev, s_len, r_len, n_tiles):
    tm = ROW_TILE
    n_lat = s_len // tm
    n_ctx = n_tiles - n_lat
    if rev:
        return n_tiles - 1 - n
    return jnp.where(n < n_ctx, n_lat + n, n - n_ctx)


def _lru_scan(xr, conv_w, conv_b, w_gate, b_gate, lam_row, s_len, rev, h_fwd=None, gelu=None):
    b, r, width = xr.shape
    tm = ROW_TILE
    n_tiles = r // tm
    fuse_mix = h_fwd is not None
    tile_of = lambda bi, n: _lru_tile(n, rev, s_len, r, n_tiles)
    full = lambda a: pl.BlockSpec(a.shape, lambda bi, n: (0,) * a.ndim)
    cur = pl.BlockSpec((1, tm, width), lambda bi, n: (bi, tile_of(bi, n), 0))
    extra_args, extra_specs = ([h_fwd, gelu], [cur, cur]) if fuse_mix else ([], [])
    return pl.pallas_call(
        functools.partial(_lru_scan_kernel, rev=rev, fuse_mix=fuse_mix, s_len=s_len, r_len=r, n_tiles=n_tiles),
        grid=(b, n_tiles),
        in_specs=_halo_specs(tm, width, tile_of, r)
        + [full(conv_w), full(conv_b), full(w_gate), full(b_gate), full(lam_row)] + extra_specs,
        out_specs=cur,
        out_shape=jax.ShapeDtypeStruct((b, r, width), BF16 if fuse_mix else F32),
        scratch_shapes=[pltpu.VMEM((tm + 16, width), F32), pltpu.VMEM((tm, width), F32),
                        pltpu.VMEM((tm, width), F32), pltpu.VMEM((1, width), F32)],
        compiler_params=_params("parallel", "arbitrary"),
        name="lru_bwd" if rev else "lru_fwd",
    )(xr, xr, xr, conv_w, conv_b, w_gate, b_gate, lam_row, *extra_args)


def _rope_tables(s_len):
    t = jnp.arange(s_len)
    inv = ROPE_THETA ** (-jnp.arange(ROPE_PAIRS, dtype=F32) / ROPE_PAIRS)
    ang_r = (t // GRID_W).astype(F32)[:, None] * inv
    ang_c = (t % GRID_W).astype(F32)[:, None] * inv
    cos = jnp.concatenate([jnp.cos(ang_r)] * 2 + [jnp.cos(ang_c)] * 2, axis=1)
    sin = jnp.concatenate([-jnp.sin(ang_r), jnp.sin(ang_r), -jnp.sin(ang_c), jnp.sin(ang_c)], axis=1)
    return jnp.tile(cos, (1, 2)), jnp.tile(sin, (1, 2))


def kernel(x, c, ctx, c_ctx, ev_norm1, ev_norm2, ev_ada_w, ev_ada_b, ev_w_in, ev_w_out, ev_q_norm, ev_k_norm, ev_lam_q1, ev_lam_k1, ev_lam_q2, ev_lam_k2, ev_sub_norm, ev_gdn_conv, ev_gdn_a_log, ev_gdn_dt_bias, ev_gdn_o_norm, ev_ffn_w_gu, ev_ffn_w_down, od_norm1, od_norm2, od_ada_w, od_ada_b, od_w_in, od_conv_w, od_conv_b, od_w_r, od_b_r, od_w_i, od_b_i, od_lam, od_w_out, od_ffn_w_gu, od_ffn_w_down):
    batch, s_len, d = x.shape
    assert ev_norm1.shape[0] == 1 and od_norm1.shape[0] == 1, "two-layer block: one even and one odd layer"
    assert ctx.shape[1] % ROW_TILE == 0 and s_len % ROW_TILE == 0 and batch + 1 <= 8
    row = lambda v: v.reshape(1, -1).astype(F32)

    h = jnp.concatenate([x, ctx], axis=1)
    cond = jnp.zeros((8, d), F32).at[:batch].set(c).at[batch].set(c_ctx)

    mod0 = _mod_table(_ada_mod(cond, ev_ada_w[0], ev_ada_b[0]), batch, d)
    w_in = ev_w_in[0].astype(BF16)
    qw = DA_HEADS * 2 * DA_QK
    vw = DA_HEADS * DA_V
    gw = GDN_HEADS * GDN_DK
    o_da, o_gdn = 2 * qw + vw, 2 * qw + vw + 3 * gw
    wq, wk, wv = w_in[:, :qw], w_in[:, qw:2 * qw], w_in[:, 2 * qw:o_da]
    wg, wz = w_in[:, o_da:o_gdn], w_in[:, o_gdn:o_gdn + gw]
    wgt = jnp.pad(w_in[:, o_gdn + gw:], ((0, 0), (0, 128 - 4 * GDN_HEADS)))
    grp = jnp.arange(qw) // DA_QK
    gsum = (grp[:, None] == grp[None, :]).astype(BF16)
    qgain = jnp.tile(ev_q_norm[0], qw // DA_QK).reshape(1, qw)
    kgain = jnp.tile(ev_k_norm[0], qw // DA_QK).reshape(1, qw)
    cos, sin = _rope_tables(s_len)
    qz, k, vt, ug, z, gt = _inproj0(h, mod0, row(ev_norm1[0]), wq, wk, wv, wg, wz, wgt, gsum, qgain, kgain,
                                    cos, sin, s_len)
    lambda_init = 0.8 - 0.6 * math.exp(-0.3 * 0)
    lam_rows = jnp.pad(jnp.stack([ev_lam_q1[0], ev_lam_k1[0], ev_lam_q2[0], ev_lam_k2[0]]),
                       ((0, 0), (0, 128 - DA_QK)))
    ax = _diff_attention(qz, k, vt, ev_sub_norm[0].reshape(DA_V, 1), lam_rows, ev_q_norm[0], ev_k_norm[0], s_len,
                         lambda_init)
    pad16 = lambda v: jnp.zeros((1, 128), F32).at[0, 8:8 + 2 * GDN_HEADS].set(v.reshape(-1))
    prep = _gdn_prep(ug, gt, ev_gdn_conv[0], pad16(ev_gdn_a_log[0]), pad16(ev_gdn_dt_bias[0]), s_len)
    o_f, o_b = _gdn_scan(prep, s_len)
    o_gain = jnp.tile(ev_gdn_o_norm[0], GDN_HEADS).reshape(1, gw)
    h = _post(h, mod0, (ax, o_f, o_b, z, o_gain), row(ev_norm2[0]), ev_w_out[0].astype(BF16),
              ev_ffn_w_gu[0].astype(BF16), ev_ffn_w_down[0].astype(BF16), s_len, h.shape[1])

    mod1 = _mod_table(_ada_mod(cond, od_ada_w[0], od_ada_b[0]), batch, d)
    w_in1 = od_w_in[0].astype(BF16)
    lw = od_conv_b.shape[1]
    gelu, xr = _inproj1(h, mod1, row(od_norm1[0]), w_in1[:, :lw], w_in1[:, lw:], s_len)
    w_gate = jnp.concatenate([od_w_r[0], od_w_i[0]], axis=-1).astype(BF16)
    b_gate = jnp.stack([od_b_r[0], od_b_i[0]], axis=1)
    h_f = _lru_scan(xr, od_conv_w[0], row(od_conv_b[0]), w_gate[0], b_gate[0], row(od_lam[0, 0]), s_len, False)
    mix = _lru_scan(xr, od_conv_w[0], row(od_conv_b[0]), w_gate[1], b_gate[1], row(od_lam[0, 1]), s_len, True,
                    h_fwd=h_f, gelu=gelu)
    return _post(h, mod1, (mix,), row(od_norm2[0]), od_w_out[0].astype(BF16), od_ffn_w_gu[0].astype(BF16),
                 od_ffn_w_down[0].astype(BF16), s_len, s_len)
```

```python
import functools
import math

import jax
import jax.numpy as jnp
from jax import lax
from jax.experimental import pallas as pl
from jax.experimental.pallas import tpu as pltpu

F32 = jnp.float32
BF16 = jnp.bfloat16
EPS = 1e-6
HIGHEST = lax.Precision.HIGHEST

GRID_W = 64
ROPE_THETA = 10000.0
DA_HEADS = 4
DA_QK = 64
DA_V = 128
DA_SCALE = DA_QK ** -0.5
ROPE_PAIRS = DA_QK // 4
GDN_HEADS = 4
GDN_DK = 128
GDN_CHUNK = 64
GDN_STACK = GDN_HEADS * GDN_CHUNK
LRU_BLOCKS = 8
LRU_C = 8.0
LOG2E = math.log2(math.e)

ROW_TILE = 256
VMEM_LIMIT = 56 * 1024 * 1024


def _mxu(a, b):
    return jnp.dot(a, b, preferred_element_type=F32)


def _mxu_nt(a, b):
    return lax.dot_general(a, b, (((1,), (1,)), ((), ())), preferred_element_type=F32)


def _dot(a, b):
    return _mxu(a.astype(BF16), b.astype(BF16))


def _dot_nt(a, b):
    return _mxu_nt(a.astype(BF16), b.astype(BF16))


def _dot_hi(a, b):
    return jnp.dot(a, b, precision=HIGHEST, preferred_element_type=F32)


def _split(a):
    hi = a.astype(BF16)
    return hi, (a - hi.astype(F32)).astype(BF16)


def _dot3(a, b):
    a_hi, a_lo = _split(a)
    b_hi, b_lo = _split(b)
    return _mxu(a_hi, b_hi) + (_mxu(a_hi, b_lo) + _mxu(a_lo, b_hi))


def _sigmoid(x):
    return 0.5 * jnp.tanh(0.5 * x) + 0.5


def _silu(x):
    return x * jax.nn.sigmoid(x)


def _modulate(x, gain, shift, scale):
    ms = jnp.mean(x * x, axis=-1, keepdims=True)
    return x * lax.rsqrt(ms + EPS) * gain * (1.0 + scale) + shift


def _params(*sem):
    return pltpu.CompilerParams(dimension_semantics=sem, vmem_limit_bytes=VMEM_LIMIT)


def _ada_kernel(c_ref, w_ref, b_ref, o_ref):
    o_ref[...] = _dot_hi(_silu(c_ref[...]), w_ref[...]) + b_ref[...]


def _ada_mod(cond, w, b):
    d = w.shape[0]
    n = w.shape[1]
    return pl.pallas_call(
        _ada_kernel,
        grid=(n // d,),
        in_specs=[pl.BlockSpec((8, d), lambda j: (0, 0)),
                  pl.BlockSpec((d, d), lambda j: (0, j)),
                  pl.BlockSpec((1, d), lambda j: (0, j))],
        out_specs=pl.BlockSpec((8, d), lambda j: (0, j)),
        out_shape=jax.ShapeDtypeStruct((8, n), F32),
        compiler_params=_params("parallel"),
        name="ada_mod",
    )(cond, w, b.reshape(1, n))


def _mod_table(m, batch, d):
    six = m.reshape(8, 6, d)
    lat = six[:batch]
    ctx = jnp.broadcast_to(six[batch][None], (batch, 6, d))
    t = jnp.stack([lat, ctx], axis=1)
    return jnp.pad(t, ((0, 0), (0, 0), (0, 2), (0, 0)))


def _rope(x, cos, sin_signed):
    n = x.shape[-1]
    lane = lax.broadcasted_iota(jnp.int32, x.shape, 1) % (2 * ROPE_PAIRS)
    partner = jnp.where(lane < ROPE_PAIRS, pltpu.roll(x, n - ROPE_PAIRS, 1), pltpu.roll(x, ROPE_PAIRS, 1))
    return x * cos + partner * sin_signed


def _inproj0_kernel(x_ref, c_ref, mod_ref, n1_ref, wq_ref, wk_ref, wv_ref, wg_ref, wz_ref, wgt_ref, gsum_ref,
                    qg_ref, kg_ref, cos_ref, sin_ref,
                    qz_ref, k_ref, vt_ref, ug_ref, z_ref, gt_ref, *, n_lat_tiles):
    is_lat = pl.program_id(1) < n_lat_tiles
    h_in = jnp.where(is_lat, x_ref[0], c_ref[0])
    y = _modulate(h_in, n1_ref[...], mod_ref[0, 0, 0:1, :], mod_ref[0, 0, 1:2, :])
    yb = y.astype(BF16)
    cos = jnp.concatenate([cos_ref[...]] * DA_HEADS, axis=1)
    sin = jnp.concatenate([sin_ref[...]] * DA_HEADS, axis=1)

    def head_norm_rope(w_ref, gain_ref):
        u = _dot(yb, w_ref[...])
        sq = u * u
        hi = sq.astype(BF16)
        lo = (sq - hi.astype(F32)).astype(BF16)
        ss = _dot(hi, gsum_ref[...]) + _dot(lo, gsum_ref[...])
        un = u * lax.rsqrt(ss * (1.0 / DA_QK) + EPS) * gain_ref[...]
        return jnp.where(is_lat, _rope(un, cos, sin), un)

    q = head_norm_rope(wq_ref, qg_ref) * (DA_SCALE * LOG2E)
    k = head_norm_rope(wk_ref, kg_ref)
    low = lax.broadcasted_iota(jnp.int32, (q.shape[0], DA_V), 1) < DA_QK
    parts = []
    for hd in range(DA_HEADS):
        qh = q[:, hd * DA_V:(hd + 1) * DA_V]
        parts += [jnp.where(low, qh, 0.0), jnp.where(low, 0.0, qh)]
    qz_ref[0] = jnp.concatenate(parts, axis=1).astype(BF16)
    k_ref[0] = k.astype(BF16)
    vt_ref[0] = _dot(yb, wv_ref[...]).T.astype(BF16)
    ug_ref[0] = _dot(yb, wg_ref[...])
    z_ref[0] = _dot(yb, wz_ref[...])
    gt_ref[0] = _dot(yb, wgt_ref[...])


def _inproj0(x, ctx, mod, norm1, wq, wk, wv, wg, wz, wgt, gsum, qgain, kgain, cos, sin):
    b, s_len, d = x.shape
    r = s_len + ctx.shape[1]
    tm = ROW_TILE
    n_lat = s_len // tm
    full = lambda a: pl.BlockSpec(a.shape, lambda bi, i: (0,) * a.ndim)
    rows = lambda w: pl.BlockSpec((1, tm, w), lambda bi, i: (bi, i, 0))
    tab = pl.BlockSpec((tm, 128), lambda bi, i: (jnp.minimum(i, n_lat - 1), 0))
    return pl.pallas_call(
        functools.partial(_inproj0_kernel, n_lat_tiles=n_lat),
        grid=(b, r // tm),
        in_specs=_split_row_specs(tm, d, n_lat)
        + [pl.BlockSpec((1, 1, 8, d), lambda bi, i: (bi, jnp.where(i < n_lat, 0, 1), 0, 0)),
           full(norm1), full(wq), full(wk), full(wv), full(wg), full(wz), full(wgt), full(gsum),
           full(qgain), full(kgain), tab, tab],
        out_specs=[rows(2 * DA_HEADS * DA_V), rows(DA_HEADS * DA_V),
                   pl.BlockSpec((1, DA_HEADS * DA_V, tm), lambda bi, i: (bi, 0, i)),
                   rows(wg.shape[1]), rows(wz.shape[1]), rows(128)],
        out_shape=[jax.ShapeDtypeStruct((b, r, 2 * DA_HEADS * DA_V), BF16),
                   jax.ShapeDtypeStruct((b, r, DA_HEADS * DA_V), BF16),
                   jax.ShapeDtypeStruct((b, DA_HEADS * DA_V, r), BF16),
                   jax.ShapeDtypeStruct((b, r, wg.shape[1]), F32),
                   jax.ShapeDtypeStruct((b, r, wz.shape[1]), F32),
                   jax.ShapeDtypeStruct((b, r, 128), F32)],
        compiler_params=_params("parallel", "parallel"),
        name="inproj0",
    )(x, ctx, mod, norm1, wq, wk, wv, wg, wz, wgt, gsum, qgain, kgain, cos, sin)


def _attn_kernel(*refs, tk, n_main, tail, lambda_init, online):
    qz_ref, k_ref, vt_ref, sub_ref, lam_ref = refs[:5]
    o_ref, acc_ref, m_ref, l_ref, s_ref = refs[-5:]
    m_ref[...] = jnp.full(m_ref.shape, -1e30, F32)
    l_ref[...] = jnp.zeros(l_ref.shape, F32)
    acc_ref[...] = jnp.zeros(acc_ref.shape, F32)
    qs = (qz_ref[0, :, 0:DA_V], qz_ref[0, :, DA_V:2 * DA_V])

    def first(kc, mi):
        st = _mxu_nt(kc, qs[mi])
        if online:
            return st
        p = jnp.exp2(st)
        l_ref[mi] = l_ref[mi] + jnp.sum(p, axis=0, keepdims=True)
        return p.astype(BF16)

    def second(x, vc, mi):
        if not online:
            acc_ref[mi] = acc_ref[mi] + _mxu(vc, x)
            return
        m_old = m_ref[mi]
        m_new = jnp.maximum(m_old, jnp.max(x, axis=0, keepdims=True))
        alpha = jnp.exp2(m_old - m_new)
        p = jnp.exp2(x - m_new)
        l_ref[mi] = l_ref[mi] * alpha + jnp.sum(p, axis=0, keepdims=True)
        acc_ref[mi] = acc_ref[mi] * alpha + _mxu(vc, p.astype(BF16))
        m_ref[mi] = m_new

    def qk(j, slot):
        kc = k_ref[0, pl.ds(pl.multiple_of(j * tk, tk), tk), :]
        for mi in range(2):
            s_ref[slot, mi] = first(kc, mi)

    def consume(j, slot):
        vc = vt_ref[0, :, pl.ds(pl.multiple_of(j * tk, tk), tk)]
        for mi in range(2):
            second(s_ref[slot, mi], vc, mi)

    def body(jj, carry):
        j = 2 * jj
        qk(j + 1, 1)
        consume(j, 0)
        qk(j + 2, 0)
        consume(j + 1, 1)
        return carry

    if n_main:
        qk(0, 0)
        lax.fori_loop(0, n_main // 2 - 1, body, 0)
        qk(n_main - 1, 1)
        consume(n_main - 2, 0)
        consume(n_main - 1, 1)
    if tail is not None:
        t0, tn = tail
        kc = k_ref[0, t0:t0 + tn, :]
        vc = vt_ref[0, :, t0:t0 + tn]
        for mi in range(2):
            second(first(kc, mi), vc, mi)

    lam_rows = lam_ref[...]
    e1 = jnp.exp(jnp.sum(lam_rows[0:1] * lam_rows[1:2], axis=-1, keepdims=True))
    e2 = jnp.exp(jnp.sum(lam_rows[2:3] * lam_rows[3:4], axis=-1, keepdims=True))
    lam = e1 - e2 + lambda_init
    o = acc_ref[0] / l_ref[0] - lam * (acc_ref[1] / l_ref[1])
    ms = jnp.mean(o * o, axis=0, keepdims=True)
    y = o * lax.rsqrt(ms + EPS) * sub_ref[...] * (1.0 - lambda_init)
    o_ref[0] = y.T.astype(BF16)


ATTN_TQ = 2048
ATTN_TK = 512


def _attn_call(qz, k, vt, sub_col, lam_rows, lambda_init, *, tq, q_tile0, n_q, key_rows, key_tile0, n_main, tail, name,
               online):
    b = k.shape[0]
    assert n_main % 2 == 0
    kern = functools.partial(_attn_kernel, tk=ATTN_TK, n_main=n_main, tail=tail, lambda_init=lambda_init,
                             online=online)
    return pl.pallas_call(
        kern,
        grid=(b, DA_HEADS, n_q),
        in_specs=[pl.BlockSpec((1, tq, 2 * DA_V), lambda bi, hd, i: (bi, q_tile0 + i, hd)),
                  pl.BlockSpec((1, key_rows, DA_V), lambda bi, hd, i: (bi, key_tile0, hd)),
                  pl.BlockSpec((1, DA_V, key_rows), lambda bi, hd, i: (bi, hd, key_tile0)),
                  pl.BlockSpec((DA_V, 1), lambda bi, hd, i: (0, 0)),
                  pl.BlockSpec((4, 128), lambda bi, hd, i: (0, 0))],
        out_specs=pl.BlockSpec((1, tq, DA_V), lambda bi, hd, i: (bi, i, hd)),
        out_shape=jax.ShapeDtypeStruct((b, n_q * tq, DA_HEADS * DA_V), BF16),
        scratch_shapes=[pltpu.VMEM((2, DA_V, tq), F32), pltpu.VMEM((2, 1, tq), F32),
                        pltpu.VMEM((2, 1, tq), F32),
                        pltpu.VMEM((2, 2, ATTN_TK, tq), F32 if online else BF16)],
        compiler_params=_params("parallel", "parallel", "arbitrary"),
        name=name + ("_online" if online else ""),
    )(qz, k, vt, sub_col, lam_rows)


MAX_UNSHIFTED_LOG2_SCORE = 40.0


def _diff_attention(qz, k, vt, sub_col, lam_rows, q_gain, k_gain, s_len, lambda_init):
    r = k.shape[1]
    c_len = r - s_len
    assert s_len % ATTN_TQ == 0 and s_len % ATTN_TK == 0 and s_len % c_len == 0 and c_len % 128 == 0

    def run(online):
        ax = _attn_call(qz, k, vt, sub_col, lam_rows, lambda_init, tq=ATTN_TQ, q_tile0=0, n_q=s_len // ATTN_TQ,
                        key_rows=r, key_tile0=0, n_main=s_len // ATTN_TK, tail=(s_len, c_len), name="diff_attn",
                        online=online)
        ac = _attn_call(qz, k, vt, sub_col, lam_rows, lambda_init, tq=c_len, q_tile0=s_len // c_len, n_q=1,
                        key_rows=c_len, key_tile0=s_len // c_len, n_main=0, tail=(0, c_len),
                        name="diff_attn_ctx", online=online)
        return ax, ac

    bound = 1.01 * DA_QK * DA_SCALE * LOG2E * jnp.max(jnp.abs(q_gain)) * jnp.max(jnp.abs(k_gain))
    return lax.cond(bound <= MAX_UNSHIFTED_LOG2_SCORE, lambda: run(False), lambda: run(True))


def _conv_taps(ext_ref, prev_ref, x_ref, next_ref, w_ref, tile, s_len, r_len):
    tm = x_ref.shape[1]
    r0 = tile * tm
    has_prev = jnp.logical_and(r0 != 0, r0 != s_len)
    has_next = jnp.logical_and(r0 + tm != s_len, r0 + tm != r_len)
    x = x_ref[0]
    ext_ref[0:8, :] = jnp.where(has_prev, prev_ref[0], 0.0)
    ext_ref[8:8 + tm, :] = x
    ext_ref[8 + tm:16 + tm, :] = jnp.where(has_next, next_ref[0], 0.0)
    w = w_ref[...]
    return (ext_ref[pl.ds(6, tm), :] * w[0:1] + ext_ref[pl.ds(7, tm), :] * w[1:2]
            + x * w[2:3] + ext_ref[pl.ds(9, tm), :] * w[3:4])


def _halo_specs(tm, width, tile_of, n_rows):
    blocks = n_rows // 8
    cur = lambda *g: tile_of(*g)
    return [pl.BlockSpec((1, 8, width), lambda *g: (g[0], jnp.maximum(cur(*g) * (tm // 8) - 1, 0), 0)),
            pl.BlockSpec((1, tm, width), lambda *g: (g[0], cur(*g), 0)),
            pl.BlockSpec((1, 8, width), lambda *g: (g[0], jnp.minimum((cur(*g) + 1) * (tm // 8), blocks - 1), 0))]


def _col(x, lane):
    return x[:, lane:lane + 1]


def _gdn_prep_kernel(prev_ref, x_ref, next_ref, gt_ref, cw_ref, arow_ref, dtrow_ref,
                     w_ref, u_ref, qg_ref, kdt_ref, att_ref, gl_ref, ext_ref, *, s_len, r_len):
    tile = pl.program_id(1)
    tm = x_ref.shape[1]
    hw = GDN_HEADS * GDN_DK
    qkv = _silu(_conv_taps(ext_ref, prev_ref, x_ref, next_ref, cw_ref, tile, s_len, r_len))

    def l2n(a):
        return a * lax.rsqrt(jnp.sum(a * a, axis=-1, keepdims=True) + EPS)

    qh = [l2n(qkv[:, hd * GDN_DK:(hd + 1) * GDN_DK]) * GDN_DK ** -0.5 for hd in range(GDN_HEADS)]
    kh = [l2n(qkv[:, hw + hd * GDN_DK:hw + (hd + 1) * GDN_DK]) for hd in range(GDN_HEADS)]
    vh = [qkv[:, 2 * hw + hd * GDN_DK:2 * hw + (hd + 1) * GDN_DK] for hd in range(GDN_HEADS)]

    gt = gt_ref[0]
    beta_all = jax.nn.sigmoid(gt)
    g_all = -jnp.exp(arow_ref[...]) * jax.nn.softplus(gt + dtrow_ref[...])

    c = GDN_CHUNK
    st = GDN_STACK
    rr = lax.broadcasted_iota(jnp.int32, (st, st), 0)
    cc = lax.broadcasted_iota(jnp.int32, (st, st), 1)
    same = (rr // c) == (cc // c)
    eye = rr == cc
    ident = jnp.where(eye, 1.0, 0.0)
    masks = []
    for tri in (rr >= cc, rr <= cc):
        incl = jnp.logical_and(same, tri)
        masks.append((incl, jnp.logical_and(incl, jnp.logical_not(eye))))
    joins = ([], [])
    m = 1
    while m < c:
        pair = (rr // (2 * m)) == (cc // (2 * m))
        r_hi, c_hi = (rr // m) % 2 == 1, (cc // m) % 2 == 1
        joins[0].append(jnp.logical_and(pair, jnp.logical_and(r_hi, jnp.logical_not(c_hi))))
        joins[1].append(jnp.logical_and(pair, jnp.logical_and(c_hi, jnp.logical_not(r_hi))))
        m *= 2
    row_in_chunk = lax.broadcasted_iota(jnp.int32, (c, 128), 0)
    chains = []

    for ci in range(tm // c):
        sl = slice(ci * c, (ci + 1) * c)
        stack = lambda parts: jnp.concatenate([p[sl] for p in parts], axis=0)
        q_s, k_s, v_s = stack(qh), stack(kh), stack(vh)
        g_c = g_all[sl]
        csum = g_c
        for sh in (1, 2, 4, 8, 16, 32):
            csum = csum + jnp.where(row_in_chunk >= sh, pltpu.roll(csum, sh, 0), 0.0)
        total = csum[c - 1:c]
        suffix = total - csum + g_c
        beta_c = beta_all[sl]
        for di in range(2):
            gcum = csum if di == 0 else suffix
            lanes = [GDN_HEADS * di + hd for hd in range(GDN_HEADS)]
            beta_s = jnp.concatenate([_col(beta_c, ln) for ln in lanes], axis=0)
            gc_s = jnp.concatenate([_col(gcum, 8 + ln) for ln in lanes], axis=0)
            gl_s = jnp.concatenate([jnp.broadcast_to(_col(total, 8 + ln), (c, 1)) for ln in lanes], axis=0)
            gc_row = jnp.sum(jnp.where(eye, jnp.broadcast_to(gc_s, (st, st)), 0.0), axis=0, keepdims=True)
            incl, strict = masks[di]
            dec = jnp.where(incl, jnp.exp(jnp.minimum(gc_s - gc_row, 0.0)), 0.0)
            kb_s = k_s * beta_s
            a_mat = jnp.where(strict, _dot_nt(kb_s, k_s) * dec, 0.0)
            qg_ref[0, di, ci] = (q_s * jnp.exp(gc_s)).astype(BF16)
            kdt_ref[0, di, ci] = (k_s * jnp.exp(gl_s - gc_s)).T.astype(BF16)
            att_ref[0, di, ci] = (_dot_nt(q_s, k_s) * dec).astype(BF16)
            gl_ref[0, di, ci] = jnp.concatenate(
                [jnp.broadcast_to(jnp.exp(_col(total, 8 + ln)), (8, GDN_DK)) for ln in lanes], axis=1)
            chains.append((di, ci, a_mat, jnp.concatenate([v_s * beta_s, kb_s * jnp.exp(gc_s)], axis=1)))

    t_invs = [ident - jnp.where(joins[di][0], a_mat, 0.0) for di, _, a_mat, _ in chains]
    for lvl in range(1, len(joins[0])):
        t_invs = [t - _dot(_dot(t, jnp.where(joins[di][lvl], a_mat, 0.0)), t)
                  for t, (di, _, a_mat, _) in zip(t_invs, chains)]
    resids = [ident - t - _dot3(a_mat, t) for t, (_, _, a_mat, _) in zip(t_invs, chains)]
    t_invs = [t + _dot(t, rs) for t, rs in zip(t_invs, resids)]
    for t, (di, ci, _, rhs) in zip(t_invs, chains):
        sol = _dot3(t, rhs)
        u_ref[0, di, ci] = sol[:, :GDN_DK]
        w_ref[0, di, ci] = sol[:, GDN_DK:].astype(BF16)


def _gdn_prep(ug, gt, conv_w, arow, dtrow, s_len):
    b, r, width = ug.shape
    tm = ROW_TILE
    cpt = tm // GDN_CHUNK
    nc = r // GDN_CHUNK
    st = GDN_STACK
    full = lambda a: pl.BlockSpec(a.shape, lambda bi, i: (0,) * a.ndim)
    out = lambda *tail: pl.BlockSpec((1, 2, cpt) + tail, lambda bi, i: (bi, 0, i) + (0,) * len(tail))
    shp = lambda dt, *tail: jax.ShapeDtypeStruct((b, 2, nc) + tail, dt)
    return pl.pallas_call(
        functools.partial(_gdn_prep_kernel, s_len=s_len, r_len=r),
        grid=(b, r // tm),
        in_specs=_halo_specs(tm, width, lambda bi, i: i, r)
        + [pl.BlockSpec((1, tm, 128), lambda bi, i: (bi, i, 0)), full(conv_w), full(arow), full(dtrow)],
        out_specs=[out(st, GDN_DK), out(st, GDN_DK), out(st, GDN_DK), out(GDN_DK, st), out(st, st),
                   out(8, GDN_HEADS * GDN_DK)],
        out_shape=[shp(BF16, st, GDN_DK), shp(F32, st, GDN_DK), shp(BF16, st, GDN_DK), shp(BF16, GDN_DK, st),
                   shp(BF16, st, st), shp(F32, 8, GDN_HEADS * GDN_DK)],
        scratch_shapes=[pltpu.VMEM((tm + 16, width), F32)],
        compiler_params=_params("parallel", "parallel"),
        name="gdn_prep",
    )(ug, ug, ug, gt, conv_w, arow, dtrow)


def _gdn_scan_kernel(*refs):
    ins, o_refs, s_ref = refs[:12], refs[12:14], refs[14]
    c = GDN_CHUNK
    dk = GDN_DK
    st = GDN_STACK

    @pl.when(pl.program_id(0) == 0)
    def _():
        s_ref[...] = jnp.zeros(s_ref.shape, F32)

    rr = lax.broadcasted_iota(jnp.int32, (st, GDN_HEADS * dk), 0) // c
    cc = lax.broadcasted_iota(jnp.int32, (st, GDN_HEADS * dk), 1) // dk
    block = rr == cc
    diag = lambda m: jnp.concatenate([m[hd * c:(hd + 1) * c, hd * dk:(hd + 1) * dk] for hd in range(GDN_HEADS)],
                                     axis=0)
    for bi in range(s_ref.shape[0]):
        for di in range(2):
            w_ref, u_ref, qg_ref, kdt_ref, att_ref, gl_ref = ins[6 * di:6 * di + 6]
            s_all = s_ref[bi, di]
            r1 = _dot(jnp.concatenate([w_ref[bi, 0, 0], qg_ref[bi, 0, 0]], axis=0), s_all)
            v_new = u_ref[bi, 0, 0] - diag(r1[:st])
            v_bd = jnp.where(block, jnp.concatenate([v_new] * GDN_HEADS, axis=1), 0.0)
            r2 = _dot(jnp.concatenate([att_ref[bi, 0, 0], kdt_ref[bi, 0, 0]], axis=0), v_bd)
            o_s = diag(r1[st:]) + diag(r2[:st])
            o_refs[di][bi] = jnp.concatenate([o_s[hd * c:(hd + 1) * c] for hd in range(GDN_HEADS)], axis=1)
            s_ref[bi, di] = s_all * gl_ref[bi, 0, 0, 0:1, :] + r2[st:]


def _gdn_scan(prep, s_len):
    w, u, qg, kdt, att, gl = prep
    b, _, nc = w.shape[:3]
    nc_lat = s_len // GDN_CHUNK
    nc_ctx = nc - nc_lat

    def chunk(di, n):
        if di == 0:
            return jnp.where(n < nc_ctx, nc_lat + n, n - nc_ctx)
        return nc - 1 - n

    def spec(a, di):
        tail = a.shape[3:]
        return pl.BlockSpec((b, 1, 1) + tail, lambda n: (0, di, chunk(di, n)) + (0,) * len(tail))

    hv = GDN_HEADS * GDN_DK
    r = nc * GDN_CHUNK
    return pl.pallas_call(
        _gdn_scan_kernel,
        grid=(nc,),
        in_specs=[spec(a, di) for di in range(2) for a in (w, u, qg, kdt, att, gl)],
        out_specs=[pl.BlockSpec((b, GDN_CHUNK, hv), lambda n, di=di: (0, chunk(di, n), 0)) for di in range(2)],
        out_shape=[jax.ShapeDtypeStruct((b, r, hv), F32)] * 2,
        scratch_shapes=[pltpu.VMEM((b, 2, GDN_DK, hv), F32)],
        compiler_params=_params("arbitrary"),
        name="gdn_scan",
    )(*([w, u, qg, kdt, att, gl] * 2))


def _post_kernel(*refs, gdn_mix, hidden, th, n_lat_tiles):
    if gdn_mix:
        (x_ref, c_ref, mod_ref, axl_ref, axc_ref, of_ref, ob_ref, z_ref, on_ref, n2_ref, wo_ref, wgu_ref, wd_ref,
         o_ref) = refs
        is_lat = pl.program_id(1) < n_lat_tiles
        h_in = jnp.where(is_lat, x_ref[0], c_ref[0])
        ax = jnp.where(is_lat, axl_ref[0], axc_ref[0])
    else:
        h_ref, mod_ref, mix_ref, n2_ref, wo_ref, wgu_ref, wd_ref, o_ref = refs
        h_in = h_ref[0]
    mod = mod_ref[0, 0]
    gate1, shift2, scale2, gate2 = mod[2:3], mod[3:4], mod[4:5], mod[5:6]
    if gdn_mix:
        o = of_ref[0] + ob_ref[0]
        parts = []
        for hd in range(GDN_HEADS):
            oh = o[:, hd * GDN_DK:(hd + 1) * GDN_DK]
            parts.append(oh * lax.rsqrt(jnp.mean(oh * oh, axis=-1, keepdims=True) + EPS))
        bx = jnp.concatenate(parts, axis=1) * on_ref[...] * _silu(z_ref[0])
        half = ax.shape[1]
        mixed = _dot(ax, wo_ref[0:half, :]) + _dot(bx, wo_ref[half:, :])
    else:
        mixed = _dot(mix_ref[0], wo_ref[...])
    h1 = h_in + gate1 * mixed
    y = _modulate(h1, n2_ref[...], shift2, scale2).astype(BF16)
    acts = []
    for j in range(hidden // th):
        g = _dot(y, wgu_ref[:, j * th:(j + 1) * th])
        u = _dot(y, wgu_ref[:, hidden + j * th:hidden + (j + 1) * th])
        acts.append((_silu(g) * u).astype(BF16))
    acc = _mxu(jnp.concatenate(acts, axis=1), wd_ref[...])
    o_ref[0] = h1 + gate2 * acc


def _split_row_specs(tm, width, n_lat):
    return [pl.BlockSpec((1, tm, width), lambda bi, i: (bi, jnp.minimum(i, n_lat - 1), 0)),
            pl.BlockSpec((1, tm, width), lambda bi, i: (bi, jnp.maximum(i - n_lat, 0), 0))]


def _post(h, mod, mix_inputs, norm2, w_out, w_gu, w_down, s_len, rows_out):
    gdn_mix = len(mix_inputs) > 1
    b, _, d = (h[0] if gdn_mix else h).shape
    tm = ROW_TILE
    n_lat = s_len // tm
    hidden = w_down.shape[0]
    full = lambda a: pl.BlockSpec(a.shape, lambda bi, i: (0,) * a.ndim)
    rows = lambda w: pl.BlockSpec((1, tm, w), lambda bi, i: (bi, i, 0))
    mod_spec = pl.BlockSpec((1, 1, 8, d), lambda bi, i: (bi, jnp.where(i < n_lat, 0, 1), 0, 0))
    if gdn_mix:
        ax, o_f, o_b, z, o_gain = mix_inputs
        args = [*h, mod, *ax, o_f, o_b, z, o_gain]
        specs = (_split_row_specs(tm, d, n_lat) + [mod_spec] + _split_row_specs(tm, ax[0].shape[2], n_lat)
                 + [rows(o_f.shape[2]), rows(o_b.shape[2]), rows(z.shape[2]), full(o_gain)])
    else:
        args = [h, mod, *mix_inputs]
        specs = [rows(d), mod_spec, rows(mix_inputs[0].shape[2])]
    return pl.pallas_call(
        functools.partial(_post_kernel, gdn_mix=gdn_mix, hidden=hidden, th=256, n_lat_tiles=n_lat),
        grid=(b, rows_out // tm),
        in_specs=specs + [full(norm2), full(w_out), full(w_gu), full(w_down)],
        out_specs=rows(d),
        out_shape=jax.ShapeDtypeStruct((b, rows_out, d), F32),
        compiler_params=_params("parallel", "parallel"),
        name="post_gdn" if gdn_mix else "post_lru",
    )(*args, norm2, w_out, w_gu, w_down)


def _inproj1_kernel(h_ref, mod_ref, n1_ref, wa_ref, wb_ref, gl_ref, xr_ref):
    y = _modulate(h_ref[0], n1_ref[...], mod_ref[0, 0, 0:1, :], mod_ref[0, 0, 1:2, :]).astype(BF16)
    gl_ref[0] = jax.nn.gelu(_dot(y, wa_ref[...]))
    xr_ref[0] = _dot(y, wb_ref[...])


def _inproj1(h, mod, norm1, wa, wb, s_len):
    b, r, d = h.shape
    tm = ROW_TILE
    n_lat = s_len // tm
    full = lambda a: pl.BlockSpec(a.shape, lambda bi, i: (0,) * a.ndim)
    rows = lambda w: pl.BlockSpec((1, tm, w), lambda bi, i: (bi, i, 0))
    return pl.pallas_call(
        _inproj1_kernel,
        grid=(b, r // tm),
        in_specs=[rows(d), pl.BlockSpec((1, 1, 8, d), lambda bi, i: (bi, jnp.where(i < n_lat, 0, 1), 0, 0)),
                  full(norm1), full(wa), full(wb)],
        out_specs=[rows(wa.shape[1]), rows(wb.shape[1])],
        out_shape=[jax.ShapeDtypeStruct((b, r, wa.shape[1]), F32), jax.ShapeDtypeStruct((b, r, wb.shape[1]), F32)],
        compiler_params=_params("parallel", "parallel"),
        name="inproj1",
    )(h, mod, norm1, wa, wb)


def _lru_scan_kernel(*refs, rev, fuse_mix, s_len, r_len, n_tiles):
    if fuse_mix:
        (prev_ref, x_ref, next_ref, cw_ref, cb_ref, wg_ref, bg_ref, lam_ref, hf_ref, gl_ref,
         o_ref, ext_ref, a_ref, b_ref, carry_ref) = refs
    else:
        (prev_ref, x_ref, next_ref, cw_ref, cb_ref, wg_ref, bg_ref, lam_ref,
         o_ref, ext_ref, a_ref, b_ref, carry_ref) = refs
    n = pl.program_id(1)
    tile = _lru_tile(n, rev, s_len, r_len, n_tiles)
    tm = x_ref.shape[1]
    bw = wg_ref.shape[1]

    @pl.when(n == 0)
    def _():
        carry_ref[...] = jnp.zeros(carry_ref.shape, F32)

    xc = _conv_taps(ext_ref, prev_ref, x_ref, next_ref, cw_ref, tile, s_len, r_len) + cb_ref[...]
    res = [_dot(xc[:, kb * bw:(kb + 1) * bw], wg_ref[kb]) for kb in range(LRU_BLOCKS)]
    r_gate = _sigmoid(jnp.concatenate([t[:, :bw] for t in res], axis=1) + bg_ref[0:1])
    i_gate = _sigmoid(jnp.concatenate([t[:, bw:] for t in res], axis=1) + bg_ref[1:2])
    log_a = -LRU_C * r_gate * jax.nn.softplus(-lam_ref[...])
    a = jnp.exp(log_a)
    a_ref[...] = a
    b_ref[...] = jnp.sqrt(-jnp.tanh(log_a) * (a * a + 1.0)) * (i_gate * xc)

    row = lax.broadcasted_iota(jnp.int32, (8, a_ref.shape[1]), 0)
    groups = tm // 8

    def group(gi, carry):
        g0 = pl.multiple_of((groups - 1 - gi if rev else gi) * 8, 8)
        acc_a = a_ref[pl.ds(g0, 8), :]
        acc_b = b_ref[pl.ds(g0, 8), :]
        for sh in (1, 2, 4):
            keep = (row < 8 - sh) if rev else (row >= sh)
            amt = 8 - sh if rev else sh
            sh_b = pltpu.roll(acc_b, amt, 0)
            sh_a = pltpu.roll(acc_a, amt, 0)
            acc_b = jnp.where(keep, acc_a * sh_b + acc_b, acc_b)
            acc_a = jnp.where(keep, acc_a * sh_a, acc_a)
        hs = acc_a * carry + acc_b
        if fuse_mix:
            o_ref[0, pl.ds(g0, 8), :] = (gl_ref[0, pl.ds(g0, 8), :] * (hf_ref[0, pl.ds(g0, 8), :] + hs)).astype(BF16)
        else:
            o_ref[0, pl.ds(g0, 8), :] = hs
        return hs[0:1] if rev else hs[7:8]

    carry_ref[...] = lax.fori_loop(0, groups, group, carry_ref[...], unroll=4)


def _lru_tile(n, rev, s_len, r_len, n_tiles):
    tm = ROW_TILE
    n_lat = s_len // tm
    n_ctx = n_tiles - n_lat
    if rev:
        return n_tiles - 1 - n
    return jnp.where(n < n_ctx, n_lat + n, n - n_ctx)


def _lru_scan(xr, conv_w, conv_b, w_gate, b_gate, lam_row, s_len, rev, h_fwd=None, gelu=None):
    b, r, width = xr.shape
    tm = ROW_TILE
    n_tiles = r // tm
    fuse_mix = h_fwd is not None
    tile_of = lambda bi, n: _lru_tile(n, rev, s_len, r, n_tiles)
    full = lambda a: pl.BlockSpec(a.shape, lambda bi, n: (0,) * a.ndim)
    cur = pl.BlockSpec((1, tm, width), lambda bi, n: (bi, tile_of(bi, n), 0))
    extra_args, extra_specs = ([h_fwd, gelu], [cur, cur]) if fuse_mix else ([], [])
    return pl.pallas_call(
        functools.partial(_lru_scan_kernel, rev=rev, fuse_mix=fuse_mix, s_len=s_len, r_len=r, n_tiles=n_tiles),
        grid=(b, n_tiles),
        in_specs=_halo_specs(tm, width, tile_of, r)
        + [full(conv_w), full(conv_b), full(w_gate), full(b_gate), full(lam_row)] + extra_specs,
        out_specs=cur,
        out_shape=jax.ShapeDtypeStruct((b, r, width), BF16 if fuse_mix else F32),
        scratch_shapes=[pltpu.VMEM((tm + 16, width), F32), pltpu.VMEM((tm, width), F32),
                        pltpu.VMEM((tm, width), F32), pltpu.VMEM((1, width), F32)],
        compiler_params=_params("parallel", "arbitrary"),
        name="lru_bwd" if rev else "lru_fwd",
    )(xr, xr, xr, conv_w, conv_b, w_gate, b_gate, lam_row, *extra_args)


def _rope_tables(s_len):
    t = jnp.arange(s_len)
    inv = ROPE_THETA ** (-jnp.arange(ROPE_PAIRS, dtype=F32) / ROPE_PAIRS)
    ang_r = (t // GRID_W).astype(F32)[:, None] * inv
    ang_c = (t % GRID_W).astype(F32)[:, None] * inv
    cos = jnp.concatenate([jnp.cos(ang_r)] * 2 + [jnp.cos(ang_c)] * 2, axis=1)
    sin = jnp.concatenate([-jnp.sin(ang_r), jnp.sin(ang_r), -jnp.sin(ang_c), jnp.sin(ang_c)], axis=1)
    return jnp.tile(cos, (1, 2)), jnp.tile(sin, (1, 2))


def kernel(x, c, ctx, c_ctx, ev_norm1, ev_norm2, ev_ada_w, ev_ada_b, ev_w_in, ev_w_out, ev_q_norm, ev_k_norm, ev_lam_q1, ev_lam_k1, ev_lam_q2, ev_lam_k2, ev_sub_norm, ev_gdn_conv, ev_gdn_a_log, ev_gdn_dt_bias, ev_gdn_o_norm, ev_ffn_w_gu, ev_ffn_w_down, od_norm1, od_norm2, od_ada_w, od_ada_b, od_w_in, od_conv_w, od_conv_b, od_w_r, od_b_r, od_w_i, od_b_i, od_lam, od_w_out, od_ffn_w_gu, od_ffn_w_down):
    batch, s_len, d = x.shape
    assert ev_norm1.shape[0] == 1 and od_norm1.shape[0] == 1, "two-layer block: one even and one odd layer"
    assert ctx.shape[1] % ROW_TILE == 0 and s_len % ROW_TILE == 0 and batch + 1 <= 8
    row = lambda v: v.reshape(1, -1).astype(F32)

    cond = jnp.zeros((8, d), F32).at[:batch].set(c).at[batch].set(c_ctx)

    mod0 = _mod_table(_ada_mod(cond, ev_ada_w[0], ev_ada_b[0]), batch, d)
    w_in = ev_w_in[0].astype(BF16)
    qw = DA_HEADS * 2 * DA_QK
    vw = DA_HEADS * DA_V
    gw = GDN_HEADS * GDN_DK
    o_da, o_gdn = 2 * qw + vw, 2 * qw + vw + 3 * gw
    wq, wk, wv = w_in[:, :qw], w_in[:, qw:2 * qw], w_in[:, 2 * qw:o_da]
    wg, wz = w_in[:, o_da:o_gdn], w_in[:, o_gdn:o_gdn + gw]
    wgt = jnp.pad(w_in[:, o_gdn + gw:], ((0, 0), (0, 128 - 4 * GDN_HEADS)))
    grp = jnp.arange(qw) // DA_QK
    gsum = (grp[:, None] == grp[None, :]).astype(BF16)
    qgain = jnp.tile(ev_q_norm[0], qw // DA_QK).reshape(1, qw)
    kgain = jnp.tile(ev_k_norm[0], qw // DA_QK).reshape(1, qw)
    cos, sin = _rope_tables(s_len)
    qz, k, vt, ug, z, gt = _inproj0(x, ctx, mod0, row(ev_norm1[0]), wq, wk, wv, wg, wz, wgt, gsum, qgain, kgain,
                                    cos, sin)
    lambda_init = 0.8 - 0.6 * math.exp(-0.3 * 0)
    lam_rows = jnp.pad(jnp.stack([ev_lam_q1[0], ev_lam_k1[0], ev_lam_q2[0], ev_lam_k2[0]]),
                       ((0, 0), (0, 128 - DA_QK)))
    ax = _diff_attention(qz, k, vt, ev_sub_norm[0].reshape(DA_V, 1), lam_rows, ev_q_norm[0], ev_k_norm[0], s_len,
                         lambda_init)
    pad16 = lambda v: jnp.zeros((1, 128), F32).at[0, 8:8 + 2 * GDN_HEADS].set(v.reshape(-1))
    prep = _gdn_prep(ug, gt, ev_gdn_conv[0], pad16(ev_gdn_a_log[0]), pad16(ev_gdn_dt_bias[0]), s_len)
    o_f, o_b = _gdn_scan(prep, s_len)
    o_gain = jnp.tile(ev_gdn_o_norm[0], GDN_HEADS).reshape(1, gw)
    h = _post((x, ctx), mod0, (ax, o_f, o_b, z, o_gain), row(ev_norm2[0]), ev_w_out[0].astype(BF16),
              ev_ffn_w_gu[0].astype(BF16), ev_ffn_w_down[0].astype(BF16), s_len, s_len + ctx.shape[1])

    mod1 = _mod_table(_ada_mod(cond, od_ada_w[0], od_ada_b[0]), batch, d)
    w_in1 = od_w_in[0].astype(BF16)
    lw = od_conv_b.shape[1]
    gelu, xr = _inproj1(h, mod1, row(od_norm1[0]), w_in1[:, :lw], w_in1[:, lw:], s_len)
    w_gate = jnp.concatenate([od_w_r[0], od_w_i[0]], axis=-1).astype(BF16)
    b_gate = jnp.stack([od_b_r[0], od_b_i[0]], axis=1)
    h_f = _lru_scan(xr, od_conv_w[0], row(od_conv_b[0]), w_gate[0], b_gate[0], row(od_lam[0, 0]), s_len, False)
    mix = _lru_scan(xr, od_conv_w[0], row(od_conv_b[0]), w_gate[1], b_gate[1], row(od_lam[0, 1]), s_len, True,
                    h_fwd=h_f, gelu=gelu)
    return _post(h, mod1, (mix,), row(od_norm2[0]), od_w_out[0].astype(BF16), od_ffn_w_gu[0].astype(BF16),
                 od_ffn_w_down[0].astype(BF16), s_len, s_len)
```

```python
import functools
import math

import jax
import jax.numpy as jnp
from jax import lax
from jax.experimental import pallas as pl
from jax.experimental.pallas import tpu as pltpu

F32 = jnp.float32
BF16 = jnp.bfloat16
EPS = 1e-6
HIGHEST = lax.Precision.HIGHEST

GRID_W = 64
ROPE_THETA = 10000.0
DA_HEADS = 4
DA_QK = 64
DA_V = 128
DA_SCALE = DA_QK ** -0.5
ROPE_PAIRS = DA_QK // 4
GDN_HEADS = 4
GDN_DK = 128
GDN_CHUNK = 64
GDN_STACK = GDN_HEADS * GDN_CHUNK
LRU_BLOCKS = 8
LRU_C = 8.0
LOG2E = math.log2(math.e)

ROW_TILE = 256
VMEM_LIMIT = 56 * 1024 * 1024


def _mxu(a, b):
    return jnp.dot(a, b, preferred_element_type=F32)


def _mxu_nt(a, b):
    return lax.dot_general(a, b, (((1,), (1,)), ((), ())), preferred_element_type=F32)


def _dot(a, b):
    return _mxu(a.astype(BF16), b.astype(BF16))


def _dot_nt(a, b):
    return _mxu_nt(a.astype(BF16), b.astype(BF16))


def _dot_hi(a, b):
    return jnp.dot(a, b, precision=HIGHEST, preferred_element_type=F32)


def _split(a):
    hi = a.astype(BF16)
    return hi, (a - hi.astype(F32)).astype(BF16)


def _dot3(a, b):
    a_hi, a_lo = _split(a)
    b_hi, b_lo = _split(b)
    return _mxu(a_hi, b_hi) + (_mxu(a_hi, b_lo) + _mxu(a_lo, b_hi))


def _sigmoid(x):
    return 0.5 * jnp.tanh(0.5 * x) + 0.5


def _silu(x):
    return x * jax.nn.sigmoid(x)


def _modulate(x, gain, shift, scale):
    ms = jnp.mean(x * x, axis=-1, keepdims=True)
    return x * lax.rsqrt(ms + EPS) * gain * (1.0 + scale) + shift


def _params(*sem):
    return pltpu.CompilerParams(dimension_semantics=sem, vmem_limit_bytes=VMEM_LIMIT)


def _ada_kernel(c_ref, w_ref, b_ref, o_ref):
    o_ref[...] = _dot_hi(_silu(c_ref[...]), w_ref[...]) + b_ref[...]


def _ada_mod(cond, w, b):
    d = w.shape[0]
    n = w.shape[1]
    return pl.pallas_call(
        _ada_kernel,
        grid=(n // d,),
        in_specs=[pl.BlockSpec((8, d), lambda j: (0, 0)),
                  pl.BlockSpec((d, d), lambda j: (0, j)),
                  pl.BlockSpec((1, d), lambda j: (0, j))],
        out_specs=pl.BlockSpec((8, d), lambda j: (0, j)),
        out_shape=jax.ShapeDtypeStruct((8, n), F32),
        compiler_params=_params("parallel"),
        name="ada_mod",
    )(cond, w, b.reshape(1, n))


def _mod_table(m, batch, d):
    six = m.reshape(8, 6, d)
    lat = six[:batch]
    ctx = jnp.broadcast_to(six[batch][None], (batch, 6, d))
    t = jnp.stack([lat, ctx], axis=1)
    return jnp.pad(t, ((0, 0), (0, 0), (0, 2), (0, 0)))


def _rope(x, cos, sin_signed):
    n = x.shape[-1]
    lane = lax.broadcasted_iota(jnp.int32, x.shape, 1) % (2 * ROPE_PAIRS)
    partner = jnp.where(lane < ROPE_PAIRS, pltpu.roll(x, n - ROPE_PAIRS, 1), pltpu.roll(x, ROPE_PAIRS, 1))
    return x * cos + partner * sin_signed


def _inproj0_kernel(x_ref, c_ref, mod_ref, n1_ref, wq_ref, wk_ref, wv_ref, wg_ref, wz_ref, wgt_ref, gsum_ref,
                    qg_ref, kg_ref, cos_ref, sin_ref,
                    qz_ref, k_ref, vt_ref, ug_ref, z_ref, gt_ref, *, n_lat_tiles):
    is_lat = pl.program_id(1) < n_lat_tiles
    h_in = jnp.where(is_lat, x_ref[0], c_ref[0])
    y = _modulate(h_in, n1_ref[...], mod_ref[0, 0, 0:1, :], mod_ref[0, 0, 1:2, :])
    yb = y.astype(BF16)
    cos = jnp.concatenate([cos_ref[...]] * DA_HEADS, axis=1)
    sin = jnp.concatenate([sin_ref[...]] * DA_HEADS, axis=1)

    def head_norm_rope(w_ref, gain_ref):
        u = _dot(yb, w_ref[...])
        sq = u * u
        hi = sq.astype(BF16)
        lo = (sq - hi.astype(F32)).astype(BF16)
        ss = _dot(hi, gsum_ref[...]) + _dot(lo, gsum_ref[...])
        un = u * lax.rsqrt(ss * (1.0 / DA_QK) + EPS) * gain_ref[...]
        return jnp.where(is_lat, _rope(un, cos, sin), un)

    q = head_norm_rope(wq_ref, qg_ref) * (DA_SCALE * LOG2E)
    k = head_norm_rope(wk_ref, kg_ref)
    low = lax.broadcasted_iota(jnp.int32, (q.shape[0], DA_V), 1) < DA_QK
    parts = []
    for hd in range(DA_HEADS):
        qh = q[:, hd * DA_V:(hd + 1) * DA_V]
        parts += [jnp.where(low, qh, 0.0), jnp.where(low, 0.0, qh)]
    qz_ref[0] = jnp.concatenate(parts, axis=1).astype(BF16)
    k_ref[0] = k.astype(BF16)
    vt_ref[0] = _dot(yb, wv_ref[...]).T.astype(BF16)
    ug_ref[0] = _dot(yb, wg_ref[...])
    z_ref[0] = _dot(yb, wz_ref[...])
    gt_ref[0] = _dot(yb, wgt_ref[...])


def _inproj0(x, ctx, mod, norm1, wq, wk, wv, wg, wz, wgt, gsum, qgain, kgain, cos, sin):
    b, s_len, d = x.shape
    r = s_len + ctx.shape[1]
    tm = ROW_TILE
    n_lat = s_len // tm
    full = lambda a: pl.BlockSpec(a.shape, lambda bi, i: (0,) * a.ndim)
    rows = lambda w: pl.BlockSpec((1, tm, w), lambda bi, i: (bi, i, 0))
    tab = pl.BlockSpec((tm, 128), lambda bi, i: (jnp.minimum(i, n_lat - 1), 0))
    return pl.pallas_call(
        functools.partial(_inproj0_kernel, n_lat_tiles=n_lat),
        grid=(b, r // tm),
        in_specs=_split_row_specs(tm, d, n_lat)
        + [pl.BlockSpec((1, 1, 8, d), lambda bi, i: (bi, jnp.where(i < n_lat, 0, 1), 0, 0)),
           full(norm1), full(wq), full(wk), full(wv), full(wg), full(wz), full(wgt), full(gsum),
           full(qgain), full(kgain), tab, tab],
        out_specs=[rows(2 * DA_HEADS * DA_V), rows(DA_HEADS * DA_V),
                   pl.BlockSpec((1, DA_HEADS * DA_V, tm), lambda bi, i: (bi, 0, i)),
                   rows(wg.shape[1]), rows(wz.shape[1]), rows(128)],
        out_shape=[jax.ShapeDtypeStruct((b, r, 2 * DA_HEADS * DA_V), BF16),
                   jax.ShapeDtypeStruct((b, r, DA_HEADS * DA_V), BF16),
                   jax.ShapeDtypeStruct((b, DA_HEADS * DA_V, r), BF16),
                   jax.ShapeDtypeStruct((b, r, wg.shape[1]), F32),
                   jax.ShapeDtypeStruct((b, r, wz.shape[1]), F32),
                   jax.ShapeDtypeStruct((b, r, 128), F32)],
        compiler_params=_params("parallel", "parallel"),
        name="inproj0",
    )(x, ctx, mod, norm1, wq, wk, wv, wg, wz, wgt, gsum, qgain, kgain, cos, sin)


def _attn_kernel(*refs, tk, n_main, tail, lambda_init, online):
    qz_ref, k_ref, vt_ref, sub_ref, lam_ref = refs[:5]
    o_ref, acc_ref, m_ref, l_ref, s_ref = refs[-5:]
    m_ref[...] = jnp.full(m_ref.shape, -1e30, F32)
    l_ref[...] = jnp.zeros(l_ref.shape, F32)
    acc_ref[...] = jnp.zeros(acc_ref.shape, F32)
    qs = (qz_ref[0, :, 0:DA_V], qz_ref[0, :, DA_V:2 * DA_V])

    def first(kc, mi):
        st = _mxu_nt(kc, qs[mi])
        if online:
            return st
        p = jnp.exp2(st)
        l_ref[mi] = l_ref[mi] + jnp.sum(p, axis=0, keepdims=True)
        return p.astype(BF16)

    def second(x, vc, mi):
        if not online:
            acc_ref[mi] = acc_ref[mi] + _mxu(vc, x)
            return
        m_old = m_ref[mi]
        m_new = jnp.maximum(m_old, jnp.max(x, axis=0, keepdims=True))
        alpha = jnp.exp2(m_old - m_new)
        p = jnp.exp2(x - m_new)
        l_ref[mi] = l_ref[mi] * alpha + jnp.sum(p, axis=0, keepdims=True)
        acc_ref[mi] = acc_ref[mi] * alpha + _mxu(vc, p.astype(BF16))
        m_ref[mi] = m_new

    def qk(j, slot):
        kc = k_ref[0, pl.ds(pl.multiple_of(j * tk, tk), tk), :]
        for mi in range(2):
            s_ref[slot, mi] = first(kc, mi)

    def consume(j, slot):
        vc = vt_ref[0, :, pl.ds(pl.multiple_of(j * tk, tk), tk)]
        for mi in range(2):
            second(s_ref[slot, mi], vc, mi)

    def body(jj, carry):
        j = 2 * jj
        qk(j + 1, 1)
        consume(j, 0)
        qk(j + 2, 0)
        consume(j + 1, 1)
        return carry

    if n_main:
        qk(0, 0)
        lax.fori_loop(0, n_main // 2 - 1, body, 0)
        qk(n_main - 1, 1)
        consume(n_main - 2, 0)
        consume(n_main - 1, 1)
    if tail is not None:
        t0, tn = tail
        kc = k_ref[0, t0:t0 + tn, :]
        vc = vt_ref[0, :, t0:t0 + tn]
        for mi in range(2):
            second(first(kc, mi), vc, mi)

    lam_rows = lam_ref[...]
    e1 = jnp.exp(jnp.sum(lam_rows[0:1] * lam_rows[1:2], axis=-1, keepdims=True))
    e2 = jnp.exp(jnp.sum(lam_rows[2:3] * lam_rows[3:4], axis=-1, keepdims=True))
    lam = e1 - e2 + lambda_init
    o = acc_ref[0] / l_ref[0] - lam * (acc_ref[1] / l_ref[1])
    ms = jnp.mean(o * o, axis=0, keepdims=True)
    y = o * lax.rsqrt(ms + EPS) * sub_ref[...] * (1.0 - lambda_init)
    o_ref[0] = y.T.astype(BF16)


ATTN_TQ = 2048
ATTN_TK = 512


def _attn_call(qz, k, vt, sub_col, lam_rows, lambda_init, *, tq, q_tile0, n_q, key_rows, key_tile0, n_main, tail, name,
               online):
    b = k.shape[0]
    assert n_main % 2 == 0
    kern = functools.partial(_attn_kernel, tk=ATTN_TK, n_main=n_main, tail=tail, lambda_init=lambda_init,
                             online=online)
    return pl.pallas_call(
        kern,
        grid=(b, DA_HEADS, n_q),
        in_specs=[pl.BlockSpec((1, tq, 2 * DA_V), lambda bi, hd, i: (bi, q_tile0 + i, hd)),
                  pl.BlockSpec((1, key_rows, DA_V), lambda bi, hd, i: (bi, key_tile0, hd)),
                  pl.BlockSpec((1, DA_V, key_rows), lambda bi, hd, i: (bi, hd, key_tile0)),
                  pl.BlockSpec((DA_V, 1), lambda bi, hd, i: (0, 0)),
                  pl.BlockSpec((4, 128), lambda bi, hd, i: (0, 0))],
        out_specs=pl.BlockSpec((1, tq, DA_V), lambda bi, hd, i: (bi, i, hd)),
        out_shape=jax.ShapeDtypeStruct((b, n_q * tq, DA_HEADS * DA_V), BF16),
        scratch_shapes=[pltpu.VMEM((2, DA_V, tq), F32), pltpu.VMEM((2, 1, tq), F32),
                        pltpu.VMEM((2, 1, tq), F32),
                        pltpu.VMEM((2, 2, ATTN_TK, tq), F32 if online else BF16)],
        compiler_params=_params("parallel", "parallel", "arbitrary"),
        name=name + ("_online" if online else ""),
    )(qz, k, vt, sub_col, lam_rows)


MAX_UNSHIFTED_LOG2_SCORE = 40.0


def _diff_attention(qz, k, vt, sub_col, lam_rows, q_gain, k_gain, s_len, lambda_init):
    r = k.shape[1]
    c_len = r - s_len
    assert s_len % ATTN_TQ == 0 and s_len % ATTN_TK == 0 and s_len % c_len == 0 and c_len % 128 == 0

    def run(online):
        ax = _attn_call(qz, k, vt, sub_col, lam_rows, lambda_init, tq=ATTN_TQ, q_tile0=0, n_q=s_len // ATTN_TQ,
                        key_rows=r, key_tile0=0, n_main=s_len // ATTN_TK, tail=(s_len, c_len), name="diff_attn",
                        online=online)
        ac = _attn_call(qz, k, vt, sub_col, lam_rows, lambda_init, tq=c_len, q_tile0=s_len // c_len, n_q=1,
                        key_rows=c_len, key_tile0=s_len // c_len, n_main=0, tail=(0, c_len),
                        name="diff_attn_ctx", online=online)
        return ax, ac

    bound = 1.01 * DA_QK * DA_SCALE * LOG2E * jnp.max(jnp.abs(q_gain)) * jnp.max(jnp.abs(k_gain))
    return lax.cond(bound <= MAX_UNSHIFTED_LOG2_SCORE, lambda: run(False), lambda: run(True))


def _conv_taps(ext_ref, prev_ref, x_ref, next_ref, w_ref, tile, s_len, r_len):
    tm = x_ref.shape[1]
    r0 = tile * tm
    has_prev = jnp.logical_and(r0 != 0, r0 != s_len)
    has_next = jnp.logical_and(r0 + tm != s_len, r0 + tm != r_len)
    x = x_ref[0]
    ext_ref[0:8, :] = jnp.where(has_prev, prev_ref[0], 0.0)
    ext_ref[8:8 + tm, :] = x
    ext_ref[8 + tm:16 + tm, :] = jnp.where(has_next, next_ref[0], 0.0)
    w = w_ref[...]
    return (ext_ref[pl.ds(6, tm), :] * w[0:1] + ext_ref[pl.ds(7, tm), :] * w[1:2]
            + x * w[2:3] + ext_ref[pl.ds(9, tm), :] * w[3:4])


def _halo_specs(tm, width, tile_of, n_rows):
    blocks = n_rows // 8
    cur = lambda *g: tile_of(*g)
    return [pl.BlockSpec((1, 8, width), lambda *g: (g[0], jnp.maximum(cur(*g) * (tm // 8) - 1, 0), 0)),
            pl.BlockSpec((1, tm, width), lambda *g: (g[0], cur(*g), 0)),
            pl.BlockSpec((1, 8, width), lambda *g: (g[0], jnp.minimum((cur(*g) + 1) * (tm // 8), blocks - 1), 0))]


def _col(x, lane):
    return x[:, lane:lane + 1]


def _gdn_prep_kernel(prev_ref, x_ref, next_ref, gt_ref, cw_ref, arow_ref, dtrow_ref,
                     w_ref, u_ref, qg_ref, kdt_ref, att_ref, gl_ref, ext_ref, *, s_len, r_len):
    tile = pl.program_id(1)
    tm = x_ref.shape[1]
    hw = GDN_HEADS * GDN_DK
    qkv = _silu(_conv_taps(ext_ref, prev_ref, x_ref, next_ref, cw_ref, tile, s_len, r_len))

    def l2n(a):
        return a * lax.rsqrt(jnp.sum(a * a, axis=-1, keepdims=True) + EPS)

    qh = [l2n(qkv[:, hd * GDN_DK:(hd + 1) * GDN_DK]) * GDN_DK ** -0.5 for hd in range(GDN_HEADS)]
    kh = [l2n(qkv[:, hw + hd * GDN_DK:hw + (hd + 1) * GDN_DK]) for hd in range(GDN_HEADS)]
    vh = [qkv[:, 2 * hw + hd * GDN_DK:2 * hw + (hd + 1) * GDN_DK] for hd in range(GDN_HEADS)]

    gt = gt_ref[0]
    beta_all = jax.nn.sigmoid(gt)
    g_all = -jnp.exp(arow_ref[...]) * jax.nn.softplus(gt + dtrow_ref[...])

    c = GDN_CHUNK
    st = GDN_STACK
    rr = lax.broadcasted_iota(jnp.int32, (st, st), 0)
    cc = lax.broadcasted_iota(jnp.int32, (st, st), 1)
    same = (rr // c) == (cc // c)
    eye = rr == cc
    ident = jnp.where(eye, 1.0, 0.0)
    masks = []
    for tri in (rr >= cc, rr <= cc):
        incl = jnp.logical_and(same, tri)
        masks.append((incl, jnp.logical_and(incl, jnp.logical_not(eye))))
    joins = ([], [])
    m = 1
    while m < c:
        pair = (rr // (2 * m)) == (cc // (2 * m))
        r_hi, c_hi = (rr // m) % 2 == 1, (cc // m) % 2 == 1
        joins[0].append(jnp.logical_and(pair, jnp.logical_and(r_hi, jnp.logical_not(c_hi))))
        joins[1].append(jnp.logical_and(pair, jnp.logical_and(c_hi, jnp.logical_not(r_hi))))
        m *= 2
    row_in_chunk = lax.broadcasted_iota(jnp.int32, (c, 128), 0)
    chains = []

    for ci in range(tm // c):
        sl = slice(ci * c, (ci + 1) * c)
        stack = lambda parts: jnp.concatenate([p[sl] for p in parts], axis=0)
        q_s, k_s, v_s = stack(qh), stack(kh), stack(vh)
        g_c = g_all[sl]
        csum = g_c
        for sh in (1, 2, 4, 8, 16, 32):
            csum = csum + jnp.where(row_in_chunk >= sh, pltpu.roll(csum, sh, 0), 0.0)
        total = csum[c - 1:c]
        suffix = total - csum + g_c
        beta_c = beta_all[sl]
        for di in range(2):
            gcum = csum if di == 0 else suffix
            lanes = [GDN_HEADS * di + hd for hd in range(GDN_HEADS)]
            beta_s = jnp.concatenate([_col(beta_c, ln) for ln in lanes], axis=0)
            gc_s = jnp.concatenate([_col(gcum, 8 + ln) for ln in lanes], axis=0)
            gl_s = jnp.concatenate([jnp.broadcast_to(_col(total, 8 + ln), (c, 1)) for ln in lanes], axis=0)
            gc_row = jnp.sum(jnp.where(eye, jnp.broadcast_to(gc_s, (st, st)), 0.0), axis=0, keepdims=True)
            incl, strict = masks[di]
            dec = jnp.where(incl, jnp.exp(jnp.minimum(gc_s - gc_row, 0.0)), 0.0)
            kb_s = k_s * beta_s
            a_mat = jnp.where(strict, _dot_nt(kb_s, k_s) * dec, 0.0)
            qg_ref[0, di, ci] = (q_s * jnp.exp(gc_s)).astype(BF16)
            kdt_ref[0, di, ci] = (k_s * jnp.exp(gl_s - gc_s)).T.astype(BF16)
            att_ref[0, di, ci] = (_dot_nt(q_s, k_s) * dec).astype(BF16)
            gl_ref[0, di, ci] = jnp.concatenate(
                [jnp.broadcast_to(jnp.exp(_col(total, 8 + ln)), (8, GDN_DK)) for ln in lanes], axis=1)
            chains.append((di, ci, a_mat, jnp.concatenate([v_s * beta_s, kb_s * jnp.exp(gc_s)], axis=1)))

    t_invs = [ident - jnp.where(joins[di][0], a_mat, 0.0) for di, _, a_mat, _ in chains]
    for lvl in range(1, len(joins[0])):
        t_invs = [t - _dot(_dot(t, jnp.where(joins[di][lvl], a_mat, 0.0)), t)
                  for t, (di, _, a_mat, _) in zip(t_invs, chains)]
    resids = [ident - t - _dot3(a_mat, t) for t, (_, _, a_mat, _) in zip(t_invs, chains)]
    t_invs = [t + _dot(t, rs) for t, rs in zip(t_invs, resids)]
    for t, (di, ci, _, rhs) in zip(t_invs, chains):
        sol = _dot3(t, rhs)
        u_ref[0, di, ci] = sol[:, :GDN_DK]
        w_ref[0, di, ci] = sol[:, GDN_DK:].astype(BF16)


def _gdn_prep(ug, gt, conv_w, arow, dtrow, s_len):
    b, r, width = ug.shape
    tm = ROW_TILE
    cpt = tm // GDN_CHUNK
    nc = r // GDN_CHUNK
    st = GDN_STACK
    full = lambda a: pl.BlockSpec(a.shape, lambda bi, i: (0,) * a.ndim)
    out = lambda *tail: pl.BlockSpec((1, 2, cpt) + tail, lambda bi, i: (bi, 0, i) + (0,) * len(tail))
    shp = lambda dt, *tail: jax.ShapeDtypeStruct((b, 2, nc) + tail, dt)
    return pl.pallas_call(
        functools.partial(_gdn_prep_kernel, s_len=s_len, r_len=r),
        grid=(b, r // tm),
        in_specs=_halo_specs(tm, width, lambda bi, i: i, r)
        + [pl.BlockSpec((1, tm, 128), lambda bi, i: (bi, i, 0)), full(conv_w), full(arow), full(dtrow)],
        out_specs=[out(st, GDN_DK), out(st, GDN_DK), out(st, GDN_DK), out(GDN_DK, st), out(st, st),
                   out(8, GDN_HEADS * GDN_DK)],
        out_shape=[shp(BF16, st, GDN_DK), shp(F32, st, GDN_DK), shp(BF16, st, GDN_DK), shp(BF16, GDN_DK, st),
                   shp(BF16, st, st), shp(F32, 8, GDN_HEADS * GDN_DK)],
        scratch_shapes=[pltpu.VMEM((tm + 16, width), F32)],
        compiler_params=_params("parallel", "parallel"),
        name="gdn_prep",
    )(ug, ug, ug, gt, conv_w, arow, dtrow)


def _gdn_scan_kernel(*refs):
    ins, o_refs, s_ref = refs[:12], refs[12:14], refs[14]
    c = GDN_CHUNK
    dk = GDN_DK
    st = GDN_STACK

    @pl.when(pl.program_id(0) == 0)
    def _():
        s_ref[...] = jnp.zeros(s_ref.shape, F32)

    rr = lax.broadcasted_iota(jnp.int32, (st, GDN_HEADS * dk), 0) // c
    cc = lax.broadcasted_iota(jnp.int32, (st, GDN_HEADS * dk), 1) // dk
    block = rr == cc
    diag = lambda m: jnp.concatenate([m[hd * c:(hd + 1) * c, hd * dk:(hd + 1) * dk] for hd in range(GDN_HEADS)],
                                     axis=0)
    chains = [(bi, di) for bi in range(s_ref.shape[0]) for di in range(2)]
    r1s, r2s = [], []
    for bi, di in chains:
        w_s, qg_s = ins[6 * di][bi, 0, 0], ins[6 * di + 2][bi, 0, 0]
        s_all = s_ref[bi, di]
        r1s.append([_dot(jnp.concatenate([w_s[hd * c:(hd + 1) * c], qg_s[hd * c:(hd + 1) * c]], axis=0),
                         s_all[:, hd * dk:(hd + 1) * dk]) for hd in range(GDN_HEADS)])
    for (bi, di), r1 in zip(chains, r1s):
        v_new = ins[6 * di + 1][bi, 0, 0] - jnp.concatenate([t[:c] for t in r1], axis=0)
        v_bd = jnp.where(block, jnp.concatenate([v_new] * GDN_HEADS, axis=1), 0.0)
        r2s.append(_dot(jnp.concatenate([ins[6 * di + 4][bi, 0, 0], ins[6 * di + 3][bi, 0, 0]], axis=0), v_bd))
    for (bi, di), r1, r2 in zip(chains, r1s, r2s):
        o_s = jnp.concatenate([t[c:] for t in r1], axis=0) + diag(r2[:st])
        o_refs[di][bi] = jnp.concatenate([o_s[hd * c:(hd + 1) * c] for hd in range(GDN_HEADS)], axis=1)
        s_ref[bi, di] = s_ref[bi, di] * ins[6 * di + 5][bi, 0, 0, 0:1, :] + r2[st:]


def _gdn_scan(prep, s_len):
    w, u, qg, kdt, att, gl = prep
    b, _, nc = w.shape[:3]
    nc_lat = s_len // GDN_CHUNK
    nc_ctx = nc - nc_lat

    def chunk(di, n):
        if di == 0:
            return jnp.where(n < nc_ctx, nc_lat + n, n - nc_ctx)
        return nc - 1 - n

    def spec(a, di):
        tail = a.shape[3:]
        return pl.BlockSpec((b, 1, 1) + tail, lambda n: (0, di, chunk(di, n)) + (0,) * len(tail))

    hv = GDN_HEADS * GDN_DK
    r = nc * GDN_CHUNK
    return pl.pallas_call(
        _gdn_scan_kernel,
        grid=(nc,),
        in_specs=[spec(a, di) for di in range(2) for a in (w, u, qg, kdt, att, gl)],
        out_specs=[pl.BlockSpec((b, GDN_CHUNK, hv), lambda n, di=di: (0, chunk(di, n), 0)) for di in range(2)],
        out_shape=[jax.ShapeDtypeStruct((b, r, hv), F32)] * 2,
        scratch_shapes=[pltpu.VMEM((b, 2, GDN_DK, hv), F32)],
        compiler_params=_params("arbitrary"),
        name="gdn_scan",
    )(*([w, u, qg, kdt, att, gl] * 2))


def _post_kernel(*refs, gdn_mix, hidden, th, n_lat_tiles):
    if gdn_mix:
        (x_ref, c_ref, mod_ref, axl_ref, axc_ref, of_ref, ob_ref, z_ref, on_ref, n2_ref, wo_ref, wgu_ref, wd_ref,
         o_ref) = refs
        is_lat = pl.program_id(1) < n_lat_tiles
        h_in = jnp.where(is_lat, x_ref[0], c_ref[0])
        ax = jnp.where(is_lat, axl_ref[0], axc_ref[0])
    else:
        h_ref, mod_ref, mix_ref, n2_ref, wo_ref, wgu_ref, wd_ref, o_ref = refs
        h_in = h_ref[0]
    mod = mod_ref[0, 0]
    gate1, shift2, scale2, gate2 = mod[2:3], mod[3:4], mod[4:5], mod[5:6]
    if gdn_mix:
        o = of_ref[0] + ob_ref[0]
        parts = []
        for hd in range(GDN_HEADS):
            oh = o[:, hd * GDN_DK:(hd + 1) * GDN_DK]
            parts.append(oh * lax.rsqrt(jnp.mean(oh * oh, axis=-1, keepdims=True) + EPS))
        bx = jnp.concatenate(parts, axis=1) * on_ref[...] * _silu(z_ref[0])
        half = ax.shape[1]
        mixed = _dot(ax, wo_ref[0:half, :]) + _dot(bx, wo_ref[half:, :])
    else:
        mixed = _dot(mix_ref[0], wo_ref[...])
    h1 = h_in + gate1 * mixed
    y = _modulate(h1, n2_ref[...], shift2, scale2).astype(BF16)
    gus = [(_dot(y, wgu_ref[:, j * th:(j + 1) * th]), _dot(y, wgu_ref[:, hidden + j * th:hidden + (j + 1) * th]))
           for j in range(hidden // th)]
    acts = [(_silu(g) * u).astype(BF16) for g, u in gus]
    acc = _mxu(jnp.concatenate(acts, axis=1), wd_ref[...])
    o_ref[0] = h1 + gate2 * acc


def _split_row_specs(tm, width, n_lat):
    return [pl.BlockSpec((1, tm, width), lambda bi, i: (bi, jnp.minimum(i, n_lat - 1), 0)),
            pl.BlockSpec((1, tm, width), lambda bi, i: (bi, jnp.maximum(i - n_lat, 0), 0))]


def _post(h, mod, mix_inputs, norm2, w_out, w_gu, w_down, s_len, rows_out):
    gdn_mix = len(mix_inputs) > 1
    b, _, d = (h[0] if gdn_mix else h).shape
    tm = ROW_TILE
    n_lat = s_len // tm
    hidden = w_down.shape[0]
    full = lambda a: pl.BlockSpec(a.shape, lambda bi, i: (0,) * a.ndim)
    rows = lambda w: pl.BlockSpec((1, tm, w), lambda bi, i: (bi, i, 0))
    mod_spec = pl.BlockSpec((1, 1, 8, d), lambda bi, i: (bi, jnp.where(i < n_lat, 0, 1), 0, 0))
    if gdn_mix:
        ax, o_f, o_b, z, o_gain = mix_inputs
        args = [*h, mod, *ax, o_f, o_b, z, o_gain]
        specs = (_split_row_specs(tm, d, n_lat) + [mod_spec] + _split_row_specs(tm, ax[0].shape[2], n_lat)
                 + [rows(o_f.shape[2]), rows(o_b.shape[2]), rows(z.shape[2]), full(o_gain)])
    else:
        args = [h, mod, *mix_inputs]
        specs = [rows(d), mod_spec, rows(mix_inputs[0].shape[2])]
    return pl.pallas_call(
        functools.partial(_post_kernel, gdn_mix=gdn_mix, hidden=hidden, th=256, n_lat_tiles=n_lat),
        grid=(b, rows_out // tm),
        in_specs=specs + [full(norm2), full(w_out), full(w_gu), full(w_down)],
        out_specs=rows(d),
        out_shape=jax.ShapeDtypeStruct((b, rows_out, d), F32),
        compiler_params=_params("parallel", "parallel"),
        name="post_gdn" if gdn_mix else "post_lru",
    )(*args, norm2, w_out, w_gu, w_down)


def _inproj1_kernel(h_ref, mod_ref, n1_ref, wa_ref, wb_ref, gl_ref, xr_ref):
    y = _modulate(h_ref[0], n1_ref[...], mod_ref[0, 0, 0:1, :], mod_ref[0, 0, 1:2, :]).astype(BF16)
    gl_ref[0] = jax.nn.gelu(_dot(y, wa_ref[...]))
    xr_ref[0] = _dot(y, wb_ref[...])


def _inproj1(h, mod, norm1, wa, wb, s_len):
    b, r, d = h.shape
    tm = ROW_TILE
    n_lat = s_len // tm
    full = lambda a: pl.BlockSpec(a.shape, lambda bi, i: (0,) * a.ndim)
    rows = lambda w: pl.BlockSpec((1, tm, w), lambda bi, i: (bi, i, 0))
    return pl.pallas_call(
        _inproj1_kernel,
        grid=(b, r // tm),
        in_specs=[rows(d), pl.BlockSpec((1, 1, 8, d), lambda bi, i: (bi, jnp.where(i < n_lat, 0, 1), 0, 0)),
                  full(norm1), full(wa), full(wb)],
        out_specs=[rows(wa.shape[1]), rows(wb.shape[1])],
        out_shape=[jax.ShapeDtypeStruct((b, r, wa.shape[1]), F32), jax.ShapeDtypeStruct((b, r, wb.shape[1]), F32)],
        compiler_params=_params("parallel", "parallel"),
        name="inproj1",
    )(h, mod, norm1, wa, wb)


def _lru_scan_kernel(*refs, rev, fuse_mix, s_len, r_len, n_tiles):
    if fuse_mix:
        (prev_ref, x_ref, next_ref, cw_ref, cb_ref, wg_ref, bg_ref, lam_ref, hf_ref, gl_ref,
         o_ref, ext_ref, a_ref, b_ref, carry_ref) = refs
    else:
        (prev_ref, x_ref, next_ref, cw_ref, cb_ref, wg_ref, bg_ref, lam_ref,
         o_ref, ext_ref, a_ref, b_ref, carry_ref) = refs
    n = pl.program_id(1)
    tile = _lru_tile(n, rev, s_len, r_len, n_tiles)
    tm = x_ref.shape[1]
    bw = wg_ref.shape[1]

    @pl.when(n == 0)
    def _():
        carry_ref[...] = jnp.zeros(carry_ref.shape, F32)

    xc = _conv_taps(ext_ref, prev_ref, x_ref, next_ref, cw_ref, tile, s_len, r_len) + cb_ref[...]
    res = [_dot(xc[:, kb * bw:(kb + 1) * bw], wg_ref[kb]) for kb in range(LRU_BLOCKS)]
    r_gate = _sigmoid(jnp.concatenate([t[:, :bw] for t in res], axis=1) + bg_ref[0:1])
    i_gate = _sigmoid(jnp.concatenate([t[:, bw:] for t in res], axis=1) + bg_ref[1:2])
    log_a = -LRU_C * r_gate * jax.nn.softplus(-lam_ref[...])
    a = jnp.exp(log_a)
    a_ref[...] = a
    b_ref[...] = jnp.sqrt(-jnp.tanh(log_a) * (a * a + 1.0)) * (i_gate * xc)

    row = lax.broadcasted_iota(jnp.int32, (8, a_ref.shape[1]), 0)
    groups = tm // 8

    def group(gi, carry):
        g0 = pl.multiple_of((groups - 1 - gi if rev else gi) * 8, 8)
        acc_a = a_ref[pl.ds(g0, 8), :]
        acc_b = b_ref[pl.ds(g0, 8), :]
        for sh in (1, 2, 4):
            keep = (row < 8 - sh) if rev else (row >= sh)
            amt = 8 - sh if rev else sh
            sh_b = pltpu.roll(acc_b, amt, 0)
            sh_a = pltpu.roll(acc_a, amt, 0)
            acc_b = jnp.where(keep, acc_a * sh_b + acc_b, acc_b)
            acc_a = jnp.where(keep, acc_a * sh_a, acc_a)
        hs = acc_a * carry + acc_b
        if fuse_mix:
            o_ref[0, pl.ds(g0, 8), :] = (gl_ref[0, pl.ds(g0, 8), :] * (hf_ref[0, pl.ds(g0, 8), :] + hs)).astype(BF16)
        else:
            o_ref[0, pl.ds(g0, 8), :] = hs
        return hs[0:1] if rev else hs[7:8]

    carry_ref[...] = lax.fori_loop(0, groups, group, carry_ref[...], unroll=4)


def _lru_tile(n, rev, s_len, r_len, n_tiles):
    tm = ROW_TILE
    n_lat = s_len // tm
    n_ctx = n_tiles - n_lat
    if rev:
        return n_tiles - 1 - n
    return jnp.where(n < n_ctx, n_lat + n, n - n_ctx)


def _lru_scan(xr, conv_w, conv_b, w_gate, b_gate, lam_row, s_len, rev, h_fwd=None, gelu=None):
    b, r, width = xr.shape
    tm = ROW_TILE
    n_tiles = r // tm
    fuse_mix = h_fwd is not None
    tile_of = lambda bi, n: _lru_tile(n, rev, s_len, r, n_tiles)
    full = lambda a: pl.BlockSpec(a.shape, lambda bi, n: (0,) * a.ndim)
    cur = pl.BlockSpec((1, tm, width), lambda bi, n: (bi, tile_of(bi, n), 0))
    extra_args, extra_specs = ([h_fwd, gelu], [cur, cur]) if fuse_mix else ([], [])
    return pl.pallas_call(
        functools.partial(_lru_scan_kernel, rev=rev, fuse_mix=fuse_mix, s_len=s_len, r_len=r, n_tiles=n_tiles),
        grid=(b, n_tiles),
        in_specs=_halo_specs(tm, width, tile_of, r)
        + [full(conv_w), full(conv_b), full(w_gate), full(b_gate), full(lam_row)] + extra_specs,
        out_specs=cur,
        out_shape=jax.ShapeDtypeStruct((b, r, width), BF16 if fuse_mix else F32),
        scratch_shapes=[pltpu.VMEM((tm + 16, width), F32), pltpu.VMEM((tm, width), F32),
                        pltpu.VMEM((tm, width), F32), pltpu.VMEM((1, width), F32)],
        compiler_params=_params("parallel", "arbitrary"),
        name="lru_bwd" if rev else "lru_fwd",
    )(xr, xr, xr, conv_w, conv_b, w_gate, b_gate, lam_row, *extra_args)


def _rope_tables(s_len):
    t = jnp.arange(s_len)
    inv = ROPE_THETA ** (-jnp.arange(ROPE_PAIRS, dtype=F32) / ROPE_PAIRS)
    ang_r = (t // GRID_W).astype(F32)[:, None] * inv
    ang_c = (t % GRID_W).astype(F32)[:, None] * inv
    cos = jnp.concatenate([jnp.cos(ang_r)] * 2 + [jnp.cos(ang_c)] * 2, axis=1)
    sin = jnp.concatenate([-jnp.sin(ang_r), jnp.sin(ang_r), -jnp.sin(ang_c), jnp.sin(ang_c)], axis=1)
    return jnp.tile(cos, (1, 2)), jnp.tile(sin, (1, 2))


def kernel(x, c, ctx, c_ctx, ev_norm1, ev_norm2, ev_ada_w, ev_ada_b, ev_w_in, ev_w_out, ev_q_norm, ev_k_norm, ev_lam_q1, ev_lam_k1, ev_lam_q2, ev_lam_k2, ev_sub_norm, ev_gdn_conv, ev_gdn_a_log, ev_gdn_dt_bias, ev_gdn_o_norm, ev_ffn_w_gu, ev_ffn_w_down, od_norm1, od_norm2, od_ada_w, od_ada_b, od_w_in, od_conv_w, od_conv_b, od_w_r, od_b_r, od_w_i, od_b_i, od_lam, od_w_out, od_ffn_w_gu, od_ffn_w_down):
    batch, s_len, d = x.shape
    assert ev_norm1.shape[0] == 1 and od_norm1.shape[0] == 1, "two-layer block: one even and one odd layer"
    assert ctx.shape[1] % ROW_TILE == 0 and s_len % ROW_TILE == 0 and batch + 1 <= 8
    row = lambda v: v.reshape(1, -1).astype(F32)

    cond = jnp.zeros((8, d), F32).at[:batch].set(c).at[batch].set(c_ctx)

    mod0 = _mod_table(_ada_mod(cond, ev_ada_w[0], ev_ada_b[0]), batch, d)
    w_in = ev_w_in[0].astype(BF16)
    qw = DA_HEADS * 2 * DA_QK
    vw = DA_HEADS * DA_V
    gw = GDN_HEADS * GDN_DK
    o_da, o_gdn = 2 * qw + vw, 2 * qw + vw + 3 * gw
    wq, wk, wv = w_in[:, :qw], w_in[:, qw:2 * qw], w_in[:, 2 * qw:o_da]
    wg, wz = w_in[:, o_da:o_gdn], w_in[:, o_gdn:o_gdn + gw]
    wgt = jnp.pad(w_in[:, o_gdn + gw:], ((0, 0), (0, 128 - 4 * GDN_HEADS)))
    grp = jnp.arange(qw) // DA_QK
    gsum = (grp[:, None] == grp[None, :]).astype(BF16)
    qgain = jnp.tile(ev_q_norm[0], qw // DA_QK).reshape(1, qw)
    kgain = jnp.tile(ev_k_norm[0], qw // DA_QK).reshape(1, qw)
    cos, sin = _rope_tables(s_len)
    qz, k, vt, ug, z, gt = _inproj0(x, ctx, mod0, row(ev_norm1[0]), wq, wk, wv, wg, wz, wgt, gsum, qgain, kgain,
                                    cos, sin)
    lambda_init = 0.8 - 0.6 * math.exp(-0.3 * 0)
    lam_rows = jnp.pad(jnp.stack([ev_lam_q1[0], ev_lam_k1[0], ev_lam_q2[0], ev_lam_k2[0]]),
                       ((0, 0), (0, 128 - DA_QK)))
    ax = _diff_attention(qz, k, vt, ev_sub_norm[0].reshape(DA_V, 1), lam_rows, ev_q_norm[0], ev_k_norm[0], s_len,
                         lambda_init)
    pad16 = lambda v: jnp.zeros((1, 128), F32).at[0, 8:8 + 2 * GDN_HEADS].set(v.reshape(-1))
    prep = _gdn_prep(ug, gt, ev_gdn_conv[0], pad16(ev_gdn_a_log[0]), pad16(ev_gdn_dt_bias[0]), s_len)
    o_f, o_b = _gdn_scan(prep, s_len)
    o_gain = jnp.tile(ev_gdn_o_norm[0], GDN_HEADS).reshape(1, gw)
    h = _post((x, ctx), mod0, (ax, o_f, o_b, z, o_gain), row(ev_norm2[0]), ev_w_out[0].astype(BF16),
              ev_ffn_w_gu[0].astype(BF16), ev_ffn_w_down[0].astype(BF16), s_len, s_len + ctx.shape[1])

    mod1 = _mod_table(_ada_mod(cond, od_ada_w[0], od_ada_b[0]), batch, d)
    w_in1 = od_w_in[0].astype(BF16)
    lw = od_conv_b.shape[1]
    gelu, xr = _inproj1(h, mod1, row(od_norm1[0]), w_in1[:, :lw], w_in1[:, lw:], s_len)
    w_gate = jnp.concatenate([od_w_r[0], od_w_i[0]], axis=-1).astype(BF16)
    b_gate = jnp.stack([od_b_r[0], od_b_i[0]], axis=1)
    h_f = _lru_scan(xr, od_conv_w[0], row(od_conv_b[0]), w_gate[0], b_gate[0], row(od_lam[0, 0]), s_len, False)
    mix = _lru_scan(xr, od_conv_w[0], row(od_conv_b[0]), w_gate[1], b_gate[1], row(od_lam[0, 1]), s_len, True,
                    h_fwd=h_f, gelu=gelu)
    return _post(h, mod1, (mix,), row(od_norm2[0]), od_w_out[0].astype(BF16), od_ffn_w_gu[0].astype(BF16),
                 od_ffn_w_down[0].astype(BF16), s_len, s_len)
```

```python
import functools
import math

import jax
import jax.numpy as jnp
from jax import lax
from jax.experimental import pallas as pl
from jax.experimental.pallas import tpu as pltpu

F32 = jnp.float32
BF16 = jnp.bfloat16
EPS = 1e-6
HIGHEST = lax.Precision.HIGHEST

GRID_W = 64
ROPE_THETA = 10000.0
DA_HEADS = 4
DA_QK = 64
DA_V = 128
DA_SCALE = DA_QK ** -0.5
ROPE_PAIRS = DA_QK // 4
GDN_HEADS = 4
GDN_DK = 128
GDN_CHUNK = 64
GDN_STACK = GDN_HEADS * GDN_CHUNK
LRU_BLOCKS = 8
LRU_C = 8.0
LOG2E = math.log2(math.e)

ROW_TILE = 256
VMEM_LIMIT = 56 * 1024 * 1024


def _mxu(a, b):
    return jnp.dot(a, b, preferred_element_type=F32)


def _mxu_nt(a, b):
    return lax.dot_general(a, b, (((1,), (1,)), ((), ())), preferred_element_type=F32)


def _dot(a, b):
    return _mxu(a.astype(BF16), b.astype(BF16))


def _dot_nt(a, b):
    return _mxu_nt(a.astype(BF16), b.astype(BF16))


def _dot_hi(a, b):
    return jnp.dot(a, b, precision=HIGHEST, preferred_element_type=F32)


def _split(a):
    hi = a.astype(BF16)
    return hi, (a - hi.astype(F32)).astype(BF16)


def _dot3(a, b):
    a_hi, a_lo = _split(a)
    b_hi, b_lo = _split(b)
    return _mxu(a_hi, b_hi) + (_mxu(a_hi, b_lo) + _mxu(a_lo, b_hi))


def _sigmoid(x):
    return 0.5 * jnp.tanh(0.5 * x) + 0.5


def _silu(x):
    return x * jax.nn.sigmoid(x)


def _modulate(x, gain, shift, scale):
    ms = jnp.mean(x * x, axis=-1, keepdims=True)
    return x * lax.rsqrt(ms + EPS) * gain * (1.0 + scale) + shift


def _params(*sem):
    return pltpu.CompilerParams(dimension_semantics=sem, vmem_limit_bytes=VMEM_LIMIT)


def _ada_kernel(c_ref, w_ref, b_ref, o_ref):
    o_ref[...] = _dot_hi(_silu(c_ref[...]), w_ref[...]) + b_ref[...]


def _ada_mod(cond, w, b):
    d = w.shape[0]
    n = w.shape[1]
    return pl.pallas_call(
        _ada_kernel,
        grid=(n // d,),
        in_specs=[pl.BlockSpec((8, d), lambda j: (0, 0)),
                  pl.BlockSpec((d, d), lambda j: (0, j)),
                  pl.BlockSpec((1, d), lambda j: (0, j))],
        out_specs=pl.BlockSpec((8, d), lambda j: (0, j)),
        out_shape=jax.ShapeDtypeStruct((8, n), F32),
        compiler_params=_params("parallel"),
        name="ada_mod",
    )(cond, w, b.reshape(1, n))


def _mod_table(m, batch, d):
    six = m.reshape(8, 6, d)
    lat = six[:batch]
    ctx = jnp.broadcast_to(six[batch][None], (batch, 6, d))
    t = jnp.stack([lat, ctx], axis=1)
    return jnp.pad(t, ((0, 0), (0, 0), (0, 2), (0, 0)))


def _rope(x, cos, sin_signed):
    n = x.shape[-1]
    lane = lax.broadcasted_iota(jnp.int32, x.shape, 1) % (2 * ROPE_PAIRS)
    partner = jnp.where(lane < ROPE_PAIRS, pltpu.roll(x, n - ROPE_PAIRS, 1), pltpu.roll(x, ROPE_PAIRS, 1))
    return x * cos + partner * sin_signed


def _inproj0_kernel(x_ref, c_ref, mod_ref, n1_ref, wq_ref, wk_ref, wv_ref, wg_ref, wz_ref, wgt_ref, gsum_ref,
                    qg_ref, kg_ref, cos_ref, sin_ref,
                    qz_ref, k_ref, vt_ref, ug_ref, z_ref, gt_ref, *, n_lat_tiles):
    is_lat = pl.program_id(1) < n_lat_tiles
    h_in = jnp.where(is_lat, x_ref[0], c_ref[0])
    y = _modulate(h_in, n1_ref[...], mod_ref[0, 0, 0:1, :], mod_ref[0, 0, 1:2, :])
    yb = y.astype(BF16)
    cos = jnp.concatenate([cos_ref[...]] * DA_HEADS, axis=1)
    sin = jnp.concatenate([sin_ref[...]] * DA_HEADS, axis=1)

    def head_norm_rope(w_ref, gain_ref):
        u = _dot(yb, w_ref[...])
        sq = u * u
        hi = sq.astype(BF16)
        lo = (sq - hi.astype(F32)).astype(BF16)
        ss = _dot(hi, gsum_ref[...]) + _dot(lo, gsum_ref[...])
        un = u * lax.rsqrt(ss * (1.0 / DA_QK) + EPS) * gain_ref[...]
        return jnp.where(is_lat, _rope(un, cos, sin), un)

    q = head_norm_rope(wq_ref, qg_ref) * (DA_SCALE * LOG2E)
    k = head_norm_rope(wk_ref, kg_ref)
    low = lax.broadcasted_iota(jnp.int32, (q.shape[0], DA_V), 1) < DA_QK
    parts = []
    for hd in range(DA_HEADS):
        qh = q[:, hd * DA_V:(hd + 1) * DA_V]
        parts += [jnp.where(low, qh, 0.0), jnp.where(low, 0.0, qh)]
    qz_ref[0] = jnp.concatenate(parts, axis=1).astype(BF16)
    k_ref[0] = k.astype(BF16)
    vt_ref[0] = _dot(yb, wv_ref[...]).T.astype(BF16)
    ug_ref[0] = _dot(yb, wg_ref[...])
    z_ref[0] = _dot(yb, wz_ref[...])
    gt_ref[0] = _dot(yb, wgt_ref[...])


def _inproj0(x, ctx, mod, norm1, wq, wk, wv, wg, wz, wgt, gsum, qgain, kgain, cos, sin):
    b, s_len, d = x.shape
    r = s_len + ctx.shape[1]
    tm = ROW_TILE
    n_lat = s_len // tm
    full = lambda a: pl.BlockSpec(a.shape, lambda bi, i: (0,) * a.ndim)
    rows = lambda w: pl.BlockSpec((1, tm, w), lambda bi, i: (bi, i, 0))
    tab = pl.BlockSpec((tm, 128), lambda bi, i: (jnp.minimum(i, n_lat - 1), 0))
    return pl.pallas_call(
        functools.partial(_inproj0_kernel, n_lat_tiles=n_lat),
        grid=(b, r // tm),
        in_specs=_split_row_specs(tm, d, n_lat)
        + [pl.BlockSpec((1, 1, 8, d), lambda bi, i: (bi, jnp.where(i < n_lat, 0, 1), 0, 0)),
           full(norm1), full(wq), full(wk), full(wv), full(wg), full(wz), full(wgt), full(gsum),
           full(qgain), full(kgain), tab, tab],
        out_specs=[rows(2 * DA_HEADS * DA_V), rows(DA_HEADS * DA_V),
                   pl.BlockSpec((1, DA_HEADS * DA_V, tm), lambda bi, i: (bi, 0, i)),
                   rows(wg.shape[1]), rows(wz.shape[1]), rows(128)],
        out_shape=[jax.ShapeDtypeStruct((b, r, 2 * DA_HEADS * DA_V), BF16),
                   jax.ShapeDtypeStruct((b, r, DA_HEADS * DA_V), BF16),
                   jax.ShapeDtypeStruct((b, DA_HEADS * DA_V, r), BF16),
                   jax.ShapeDtypeStruct((b, r, wg.shape[1]), F32),
                   jax.ShapeDtypeStruct((b, r, wz.shape[1]), F32),
                   jax.ShapeDtypeStruct((b, r, 128), F32)],
        compiler_params=_params("parallel", "parallel"),
        name="inproj0",
    )(x, ctx, mod, norm1, wq, wk, wv, wg, wz, wgt, gsum, qgain, kgain, cos, sin)


def _attn_kernel(*refs, tk, n_main, tail, lambda_init, online):
    qz_ref, k_ref, vt_ref, sub_ref, lam_ref = refs[:5]
    o_ref, acc_ref, m_ref, l_ref, s_ref = refs[-5:]
    m_ref[...] = jnp.full(m_ref.shape, -1e30, F32)
    l_ref[...] = jnp.zeros(l_ref.shape, F32)
    acc_ref[...] = jnp.zeros(acc_ref.shape, F32)
    qs = (qz_ref[0, :, 0:DA_V], qz_ref[0, :, DA_V:2 * DA_V])

    def first(kc, mi):
        st = _mxu_nt(kc, qs[mi])
        if online:
            return st
        p = jnp.exp2(st)
        l_ref[mi] = l_ref[mi] + jnp.sum(p, axis=0, keepdims=True)
        return p.astype(BF16)

    def second(x, vc, mi):
        if not online:
            acc_ref[mi] = acc_ref[mi] + _mxu(vc, x)
            return
        m_old = m_ref[mi]
        m_new = jnp.maximum(m_old, jnp.max(x, axis=0, keepdims=True))
        alpha = jnp.exp2(m_old - m_new)
        p = jnp.exp2(x - m_new)
        l_ref[mi] = l_ref[mi] * alpha + jnp.sum(p, axis=0, keepdims=True)
        acc_ref[mi] = acc_ref[mi] * alpha + _mxu(vc, p.astype(BF16))
        m_ref[mi] = m_new

    def qk(j, slot):
        kc = k_ref[0, pl.ds(pl.multiple_of(j * tk, tk), tk), :]
        for mi in range(2):
            s_ref[slot, mi] = first(kc, mi)

    def consume(j, slot):
        vc = vt_ref[0, :, pl.ds(pl.multiple_of(j * tk, tk), tk)]
        for mi in range(2):
            second(s_ref[slot, mi], vc, mi)

    def body(jj, carry):
        j = 2 * jj
        qk(j + 1, 1)
        consume(j, 0)
        qk(j + 2, 0)
        consume(j + 1, 1)
        return carry

    if n_main:
        qk(0, 0)
        lax.fori_loop(0, n_main // 2 - 1, body, 0)
        qk(n_main - 1, 1)
        consume(n_main - 2, 0)
        consume(n_main - 1, 1)
    if tail is not None:
        t0, tn = tail
        kc = k_ref[0, t0:t0 + tn, :]
        vc = vt_ref[0, :, t0:t0 + tn]
        for mi in range(2):
            second(first(kc, mi), vc, mi)

    lam_rows = lam_ref[...]
    e1 = jnp.exp(jnp.sum(lam_rows[0:1] * lam_rows[1:2], axis=-1, keepdims=True))
    e2 = jnp.exp(jnp.sum(lam_rows[2:3] * lam_rows[3:4], axis=-1, keepdims=True))
    lam = e1 - e2 + lambda_init
    o = acc_ref[0] / l_ref[0] - lam * (acc_ref[1] / l_ref[1])
    ms = jnp.mean(o * o, axis=0, keepdims=True)
    y = o * lax.rsqrt(ms + EPS) * sub_ref[...] * (1.0 - lambda_init)
    o_ref[0] = y.T.astype(BF16)


ATTN_TQ = 2048
ATTN_TK = 512


def _attn_call(qz, k, vt, sub_col, lam_rows, lambda_init, *, tq, q_tile0, n_q, key_rows, key_tile0, n_main, tail, name,
               online):
    b = k.shape[0]
    assert n_main % 2 == 0
    kern = functools.partial(_attn_kernel, tk=ATTN_TK, n_main=n_main, tail=tail, lambda_init=lambda_init,
                             online=online)
    return pl.pallas_call(
        kern,
        grid=(b, DA_HEADS, n_q),
        in_specs=[pl.BlockSpec((1, tq, 2 * DA_V), lambda bi, hd, i: (bi, q_tile0 + i, hd)),
                  pl.BlockSpec((1, key_rows, DA_V), lambda bi, hd, i: (bi, key_tile0, hd)),
                  pl.BlockSpec((1, DA_V, key_rows), lambda bi, hd, i: (bi, hd, key_tile0)),
                  pl.BlockSpec((DA_V, 1), lambda bi, hd, i: (0, 0)),
                  pl.BlockSpec((4, 128), lambda bi, hd, i: (0, 0))],
        out_specs=pl.BlockSpec((1, tq, DA_V), lambda bi, hd, i: (bi, i, hd)),
        out_shape=jax.ShapeDtypeStruct((b, n_q * tq, DA_HEADS * DA_V), BF16),
        scratch_shapes=[pltpu.VMEM((2, DA_V, tq), F32), pltpu.VMEM((2, 1, tq), F32),
                        pltpu.VMEM((2, 1, tq), F32),
                        pltpu.VMEM((2, 2, ATTN_TK, tq), F32 if online else BF16)],
        compiler_params=_params("parallel", "parallel", "arbitrary"),
        name=name + ("_online" if online else ""),
    )(qz, k, vt, sub_col, lam_rows)


MAX_UNSHIFTED_LOG2_SCORE = 40.0


def _diff_attention(qz, k, vt, sub_col, lam_rows, q_gain, k_gain, s_len, lambda_init):
    r = k.shape[1]
    c_len = r - s_len
    assert s_len % ATTN_TQ == 0 and s_len % ATTN_TK == 0 and s_len % c_len == 0 and c_len % 128 == 0

    def run(online):
        ax = _attn_call(qz, k, vt, sub_col, lam_rows, lambda_init, tq=ATTN_TQ, q_tile0=0, n_q=s_len // ATTN_TQ,
                        key_rows=r, key_tile0=0, n_main=s_len // ATTN_TK, tail=(s_len, c_len), name="diff_attn",
                        online=online)
        ac = _attn_call(qz, k, vt, sub_col, lam_rows, lambda_init, tq=c_len, q_tile0=s_len // c_len, n_q=1,
                        key_rows=c_len, key_tile0=s_len // c_len, n_main=0, tail=(0, c_len),
                        name="diff_attn_ctx", online=online)
        return ax, ac

    bound = 1.01 * DA_QK * DA_SCALE * LOG2E * jnp.max(jnp.abs(q_gain)) * jnp.max(jnp.abs(k_gain))
    return lax.cond(bound <= MAX_UNSHIFTED_LOG2_SCORE, lambda: run(False), lambda: run(True))


def _conv_taps(ext_ref, prev_ref, x_ref, next_ref, w_ref, tile, s_len, r_len):
    tm = x_ref.shape[1]
    r0 = tile * tm
    has_prev = jnp.logical_and(r0 != 0, r0 != s_len)
    has_next = jnp.logical_and(r0 + tm != s_len, r0 + tm != r_len)
    x = x_ref[0]
    w = w_ref[...]
    rows = lax.broadcasted_iota(jnp.int32, x.shape, 0)
    before = jnp.concatenate([jnp.where(has_prev, prev_ref[0], 0.0)] * (tm // 8), axis=0)
    after = jnp.concatenate([jnp.where(has_next, next_ref[0], 0.0)] * (tm // 8), axis=0)
    xm1 = jnp.where(rows >= 1, pltpu.roll(x, 1, 0), pltpu.roll(before, 1, 0))
    xm2 = jnp.where(rows >= 2, pltpu.roll(x, 2, 0), pltpu.roll(before, 2, 0))
    xp1 = jnp.where(rows < tm - 1, pltpu.roll(x, tm - 1, 0), pltpu.roll(after, tm - 1, 0))
    return xm2 * w[0:1] + xm1 * w[1:2] + x * w[2:3] + xp1 * w[3:4]


def _halo_specs(tm, width, tile_of, n_rows):
    blocks = n_rows // 8
    cur = lambda *g: tile_of(*g)
    return [pl.BlockSpec((1, 8, width), lambda *g: (g[0], jnp.maximum(cur(*g) * (tm // 8) - 1, 0), 0)),
            pl.BlockSpec((1, tm, width), lambda *g: (g[0], cur(*g), 0)),
            pl.BlockSpec((1, 8, width), lambda *g: (g[0], jnp.minimum((cur(*g) + 1) * (tm // 8), blocks - 1), 0))]


def _col(x, lane):
    return x[:, lane:lane + 1]


def _gdn_prep_kernel(prev_ref, x_ref, next_ref, gt_ref, cw_ref, arow_ref, dtrow_ref,
                     w_ref, u_ref, qg_ref, kdt_ref, att_ref, gl_ref, ext_ref, *, s_len, r_len):
    tile = pl.program_id(1)
    tm = x_ref.shape[1]
    hw = GDN_HEADS * GDN_DK
    qkv = _silu(_conv_taps(ext_ref, prev_ref, x_ref, next_ref, cw_ref, tile, s_len, r_len))

    def l2n(a):
        return a * lax.rsqrt(jnp.sum(a * a, axis=-1, keepdims=True) + EPS)

    qh = [l2n(qkv[:, hd * GDN_DK:(hd + 1) * GDN_DK]) * GDN_DK ** -0.5 for hd in range(GDN_HEADS)]
    kh = [l2n(qkv[:, hw + hd * GDN_DK:hw + (hd + 1) * GDN_DK]) for hd in range(GDN_HEADS)]
    vh = [qkv[:, 2 * hw + hd * GDN_DK:2 * hw + (hd + 1) * GDN_DK] for hd in range(GDN_HEADS)]

    gt = gt_ref[0]
    beta_all = jax.nn.sigmoid(gt)
    g_all = -jnp.exp(arow_ref[...]) * jax.nn.softplus(gt + dtrow_ref[...])

    c = GDN_CHUNK
    st = GDN_STACK
    rr = lax.broadcasted_iota(jnp.int32, (st, st), 0)
    cc = lax.broadcasted_iota(jnp.int32, (st, st), 1)
    same = (rr // c) == (cc // c)
    eye = rr == cc
    ident = jnp.where(eye, 1.0, 0.0)
    masks = []
    for tri in (rr >= cc, rr <= cc):
        incl = jnp.logical_and(same, tri)
        masks.append((incl, jnp.logical_and(incl, jnp.logical_not(eye))))
    joins = ([], [])
    m = 1
    while m < c:
        pair = (rr // (2 * m)) == (cc // (2 * m))
        r_hi, c_hi = (rr // m) % 2 == 1, (cc // m) % 2 == 1
        joins[0].append(jnp.logical_and(pair, jnp.logical_and(r_hi, jnp.logical_not(c_hi))))
        joins[1].append(jnp.logical_and(pair, jnp.logical_and(c_hi, jnp.logical_not(r_hi))))
        m *= 2
    row_in_chunk = lax.broadcasted_iota(jnp.int32, (c, 128), 0)
    chains = []

    for ci in range(tm // c):
        sl = slice(ci * c, (ci + 1) * c)
        stack = lambda parts: jnp.concatenate([p[sl] for p in parts], axis=0)
        q_s, k_s, v_s = stack(qh), stack(kh), stack(vh)
        g_c = g_all[sl]
        csum = g_c
        for sh in (1, 2, 4, 8, 16, 32):
            csum = csum + jnp.where(row_in_chunk >= sh, pltpu.roll(csum, sh, 0), 0.0)
        total = csum[c - 1:c]
        suffix = total - csum + g_c
        beta_c = beta_all[sl]
        for di in range(2):
            gcum = csum if di == 0 else suffix
            lanes = [GDN_HEADS * di + hd for hd in range(GDN_HEADS)]
            beta_s = jnp.concatenate([_col(beta_c, ln) for ln in lanes], axis=0)
            gc_s = jnp.concatenate([_col(gcum, 8 + ln) for ln in lanes], axis=0)
            gl_s = jnp.concatenate([jnp.broadcast_to(_col(total, 8 + ln), (c, 1)) for ln in lanes], axis=0)
            gc_row = jnp.sum(jnp.where(eye, jnp.broadcast_to(gc_s, (st, st)), 0.0), axis=0, keepdims=True)
            incl, strict = masks[di]
            dec = jnp.where(incl, jnp.exp(jnp.minimum(gc_s - gc_row, 0.0)), 0.0)
            kb_s = k_s * beta_s
            a_mat = jnp.where(strict, _dot_nt(kb_s, k_s) * dec, 0.0)
            qg_ref[0, di, ci] = (q_s * jnp.exp(gc_s)).astype(BF16)
            kdt_ref[0, di, ci] = (k_s * jnp.exp(gl_s - gc_s)).T.astype(BF16)
            att_ref[0, di, ci] = (_dot_nt(q_s, k_s) * dec).astype(BF16)
            gl_ref[0, di, ci] = jnp.concatenate(
                [jnp.broadcast_to(jnp.exp(_col(total, 8 + ln)), (8, GDN_DK)) for ln in lanes], axis=1)
            chains.append((di, ci, a_mat, jnp.concatenate([v_s * beta_s, kb_s * jnp.exp(gc_s)], axis=1)))

    t_invs = [ident - jnp.where(joins[di][0], a_mat, 0.0) for di, _, a_mat, _ in chains]
    for lvl in range(1, len(joins[0])):
        t_invs = [t - _dot(_dot(t, jnp.where(joins[di][lvl], a_mat, 0.0)), t)
                  for t, (di, _, a_mat, _) in zip(t_invs, chains)]
    resids = [ident - t - _dot3(a_mat, t) for t, (_, _, a_mat, _) in zip(t_invs, chains)]
    t_invs = [t + _dot(t, rs) for t, rs in zip(t_invs, resids)]
    for t, (di, ci, _, rhs) in zip(t_invs, chains):
        sol = _dot3(t, rhs)
        u_ref[0, di, ci] = sol[:, :GDN_DK]
        w_ref[0, di, ci] = sol[:, GDN_DK:].astype(BF16)


def _gdn_prep(ug, gt, conv_w, arow, dtrow, s_len):
    b, r, width = ug.shape
    tm = ROW_TILE
    cpt = tm // GDN_CHUNK
    nc = r // GDN_CHUNK
    st = GDN_STACK
    full = lambda a: pl.BlockSpec(a.shape, lambda bi, i: (0,) * a.ndim)
    out = lambda *tail: pl.BlockSpec((1, 2, cpt) + tail, lambda bi, i: (bi, 0, i) + (0,) * len(tail))
    shp = lambda dt, *tail: jax.ShapeDtypeStruct((b, 2, nc) + tail, dt)
    return pl.pallas_call(
        functools.partial(_gdn_prep_kernel, s_len=s_len, r_len=r),
        grid=(b, r // tm),
        in_specs=_halo_specs(tm, width, lambda bi, i: i, r)
        + [pl.BlockSpec((1, tm, 128), lambda bi, i: (bi, i, 0)), full(conv_w), full(arow), full(dtrow)],
        out_specs=[out(st, GDN_DK), out(st, GDN_DK), out(st, GDN_DK), out(GDN_DK, st), out(st, st),
                   out(8, GDN_HEADS * GDN_DK)],
        out_shape=[shp(BF16, st, GDN_DK), shp(F32, st, GDN_DK), shp(BF16, st, GDN_DK), shp(BF16, GDN_DK, st),
                   shp(BF16, st, st), shp(F32, 8, GDN_HEADS * GDN_DK)],
        scratch_shapes=[pltpu.VMEM((tm + 16, width), F32)],
        compiler_params=_params("parallel", "parallel"),
        name="gdn_prep",
    )(ug, ug, ug, gt, conv_w, arow, dtrow)


def _gdn_scan_kernel(*refs):
    ins, o_refs, s_ref = refs[:12], refs[12:14], refs[14]
    c = GDN_CHUNK
    dk = GDN_DK
    st = GDN_STACK

    @pl.when(pl.program_id(0) == 0)
    def _():
        s_ref[...] = jnp.zeros(s_ref.shape, F32)

    rr = lax.broadcasted_iota(jnp.int32, (st, GDN_HEADS * dk), 0) // c
    cc = lax.broadcasted_iota(jnp.int32, (st, GDN_HEADS * dk), 1) // dk
    block = rr == cc
    diag = lambda m: jnp.concatenate([m[hd * c:(hd + 1) * c, hd * dk:(hd + 1) * dk] for hd in range(GDN_HEADS)],
                                     axis=0)
    chains = [(bi, di) for bi in range(s_ref.shape[0]) for di in range(2)]
    r1s, r2s = [], []
    for bi, di in chains:
        w_s, qg_s = ins[6 * di][bi, 0, 0], ins[6 * di + 2][bi, 0, 0]
        s_all = s_ref[bi, di]
        r1s.append([_dot(jnp.concatenate([w_s[hd * c:(hd + 1) * c], qg_s[hd * c:(hd + 1) * c]], axis=0),
                         s_all[:, hd * dk:(hd + 1) * dk]) for hd in range(GDN_HEADS)])
    for (bi, di), r1 in zip(chains, r1s):
        v_new = ins[6 * di + 1][bi, 0, 0] - jnp.concatenate([t[:c] for t in r1], axis=0)
        v_bd = jnp.where(block, jnp.concatenate([v_new] * GDN_HEADS, axis=1), 0.0)
        r2s.append(_dot(jnp.concatenate([ins[6 * di + 4][bi, 0, 0], ins[6 * di + 3][bi, 0, 0]], axis=0), v_bd))
    for (bi, di), r1, r2 in zip(chains, r1s, r2s):
        o_s = jnp.concatenate([t[c:] for t in r1], axis=0) + diag(r2[:st])
        o_refs[di][bi] = jnp.concatenate([o_s[hd * c:(hd + 1) * c] for hd in range(GDN_HEADS)], axis=1)
        s_ref[bi, di] = s_ref[bi, di] * ins[6 * di + 5][bi, 0, 0, 0:1, :] + r2[st:]


def _gdn_scan(prep, s_len):
    w, u, qg, kdt, att, gl = prep
    b, _, nc = w.shape[:3]
    nc_lat = s_len // GDN_CHUNK
    nc_ctx = nc - nc_lat

    def chunk(di, n):
        if di == 0:
            return jnp.where(n < nc_ctx, nc_lat + n, n - nc_ctx)
        return nc - 1 - n

    def spec(a, di):
        tail = a.shape[3:]
        return pl.BlockSpec((b, 1, 1) + tail, lambda n: (0, di, chunk(di, n)) + (0,) * len(tail))

    hv = GDN_HEADS * GDN_DK
    r = nc * GDN_CHUNK
    return pl.pallas_call(
        _gdn_scan_kernel,
        grid=(nc,),
        in_specs=[spec(a, di) for di in range(2) for a in (w, u, qg, kdt, att, gl)],
        out_specs=[pl.BlockSpec((b, GDN_CHUNK, hv), lambda n, di=di: (0, chunk(di, n), 0)) for di in range(2)],
        out_shape=[jax.ShapeDtypeStruct((b, r, hv), F32)] * 2,
        scratch_shapes=[pltpu.VMEM((b, 2, GDN_DK, hv), F32)],
        compiler_params=_params("arbitrary"),
        name="gdn_scan",
    )(*([w, u, qg, kdt, att, gl] * 2))


def _post_kernel(*refs, gdn_mix, hidden, th, n_lat_tiles):
    if gdn_mix:
        (x_ref, c_ref, mod_ref, axl_ref, axc_ref, of_ref, ob_ref, z_ref, on_ref, n2_ref, wo_ref, wgu_ref, wd_ref,
         o_ref) = refs
        is_lat = pl.program_id(1) < n_lat_tiles
        h_in = jnp.where(is_lat, x_ref[0], c_ref[0])
        ax = jnp.where(is_lat, axl_ref[0], axc_ref[0])
    else:
        h_ref, mod_ref, mix_ref, n2_ref, wo_ref, wgu_ref, wd_ref, o_ref = refs
        h_in = h_ref[0]
    mod = mod_ref[0, 0]
    gate1, shift2, scale2, gate2 = mod[2:3], mod[3:4], mod[4:5], mod[5:6]
    if gdn_mix:
        o = of_ref[0] + ob_ref[0]
        parts = []
        for hd in range(GDN_HEADS):
            oh = o[:, hd * GDN_DK:(hd + 1) * GDN_DK]
            parts.append(oh * lax.rsqrt(jnp.mean(oh * oh, axis=-1, keepdims=True) + EPS))
        bx = jnp.concatenate(parts, axis=1) * on_ref[...] * _silu(z_ref[0])
        half = ax.shape[1]
        mixed = _dot(ax, wo_ref[0:half, :]) + _dot(bx, wo_ref[half:, :])
    else:
        mixed = _dot(mix_ref[0], wo_ref[...])
    h1 = h_in + gate1 * mixed
    y = _modulate(h1, n2_ref[...], shift2, scale2).astype(BF16)
    gus = [(_dot(y, wgu_ref[:, j * th:(j + 1) * th]), _dot(y, wgu_ref[:, hidden + j * th:hidden + (j + 1) * th]))
           for j in range(hidden // th)]
    acts = [(_silu(g) * u).astype(BF16) for g, u in gus]
    acc = _mxu(jnp.concatenate(acts, axis=1), wd_ref[...])
    o_ref[0] = h1 + gate2 * acc


def _split_row_specs(tm, width, n_lat):
    return [pl.BlockSpec((1, tm, width), lambda bi, i: (bi, jnp.minimum(i, n_lat - 1), 0)),
            pl.BlockSpec((1, tm, width), lambda bi, i: (bi, jnp.maximum(i - n_lat, 0), 0))]


def _post(h, mod, mix_inputs, norm2, w_out, w_gu, w_down, s_len, rows_out):
    gdn_mix = len(mix_inputs) > 1
    b, _, d = (h[0] if gdn_mix else h).shape
    tm = ROW_TILE
    n_lat = s_len // tm
    hidden = w_down.shape[0]
    full = lambda a: pl.BlockSpec(a.shape, lambda bi, i: (0,) * a.ndim)
    rows = lambda w: pl.BlockSpec((1, tm, w), lambda bi, i: (bi, i, 0))
    mod_spec = pl.BlockSpec((1, 1, 8, d), lambda bi, i: (bi, jnp.where(i < n_lat, 0, 1), 0, 0))
    if gdn_mix:
        ax, o_f, o_b, z, o_gain = mix_inputs
        args = [*h, mod, *ax, o_f, o_b, z, o_gain]
        specs = (_split_row_specs(tm, d, n_lat) + [mod_spec] + _split_row_specs(tm, ax[0].shape[2], n_lat)
                 + [rows(o_f.shape[2]), rows(o_b.shape[2]), rows(z.shape[2]), full(o_gain)])
    else:
        args = [h, mod, *mix_inputs]
        specs = [rows(d), mod_spec, rows(mix_inputs[0].shape[2])]
    return pl.pallas_call(
        functools.partial(_post_kernel, gdn_mix=gdn_mix, hidden=hidden, th=256, n_lat_tiles=n_lat),
        grid=(b, rows_out // tm),
        in_specs=specs + [full(norm2), full(w_out), full(w_gu), full(w_down)],
        out_specs=rows(d),
        out_shape=jax.ShapeDtypeStruct((b, rows_out, d), F32),
        compiler_params=_params("parallel", "parallel"),
        name="post_gdn" if gdn_mix else "post_lru",
    )(*args, norm2, w_out, w_gu, w_down)


def _inproj1_kernel(h_ref, mod_ref, n1_ref, wa_ref, wb_ref, gl_ref, xr_ref):
    y = _modulate(h_ref[0], n1_ref[...], mod_ref[0, 0, 0:1, :], mod_ref[0, 0, 1:2, :]).astype(BF16)
    gl_ref[0] = jax.nn.gelu(_dot(y, wa_ref[...]))
    xr_ref[0] = _dot(y, wb_ref[...])


def _inproj1(h, mod, norm1, wa, wb, s_len):
    b, r, d = h.shape
    tm = ROW_TILE
    n_lat = s_len // tm
    full = lambda a: pl.BlockSpec(a.shape, lambda bi, i: (0,) * a.ndim)
    rows = lambda w: pl.BlockSpec((1, tm, w), lambda bi, i: (bi, i, 0))
    return pl.pallas_call(
        _inproj1_kernel,
        grid=(b, r // tm),
        in_specs=[rows(d), pl.BlockSpec((1, 1, 8, d), lambda bi, i: (bi, jnp.where(i < n_lat, 0, 1), 0, 0)),
                  full(norm1), full(wa), full(wb)],
        out_specs=[rows(wa.shape[1]), rows(wb.shape[1])],
        out_shape=[jax.ShapeDtypeStruct((b, r, wa.shape[1]), F32), jax.ShapeDtypeStruct((b, r, wb.shape[1]), F32)],
        compiler_params=_params("parallel", "parallel"),
        name="inproj1",
    )(h, mod, norm1, wa, wb)


def _lru_scan_kernel(*refs, rev, fuse_mix, s_len, r_len, n_tiles):
    if fuse_mix:
        (prev_ref, x_ref, next_ref, cw_ref, cb_ref, wg_ref, bg_ref, lam_ref, hf_ref, gl_ref,
         o_ref, ext_ref, a_ref, b_ref, carry_ref) = refs
    else:
        (prev_ref, x_ref, next_ref, cw_ref, cb_ref, wg_ref, bg_ref, lam_ref,
         o_ref, ext_ref, a_ref, b_ref, carry_ref) = refs
    n = pl.program_id(1)
    tile = _lru_tile(n, rev, s_len, r_len, n_tiles)
    tm = x_ref.shape[1]
    bw = wg_ref.shape[1]

    @pl.when(n == 0)
    def _():
        carry_ref[...] = jnp.zeros(carry_ref.shape, F32)

    xc = _conv_taps(ext_ref, prev_ref, x_ref, next_ref, cw_ref, tile, s_len, r_len) + cb_ref[...]
    res = [_dot(xc[:, kb * bw:(kb + 1) * bw], wg_ref[kb]) for kb in range(LRU_BLOCKS)]
    r_gate = _sigmoid(jnp.concatenate([t[:, :bw] for t in res], axis=1) + bg_ref[0:1])
    i_gate = _sigmoid(jnp.concatenate([t[:, bw:] for t in res], axis=1) + bg_ref[1:2])
    log_a = -LRU_C * r_gate * jax.nn.softplus(-lam_ref[...])
    a = jnp.exp(log_a)
    a_ref[...] = a
    b_ref[...] = jnp.sqrt(-jnp.tanh(log_a) * (a * a + 1.0)) * (i_gate * xc)

    row = lax.broadcasted_iota(jnp.int32, (8, a_ref.shape[1]), 0)
    groups = tm // 8

    def group(gi, carry):
        g0 = pl.multiple_of((groups - 1 - gi if rev else gi) * 8, 8)
        acc_a = a_ref[pl.ds(g0, 8), :]
        acc_b = b_ref[pl.ds(g0, 8), :]
        for sh in (1, 2, 4):
            keep = (row < 8 - sh) if rev else (row >= sh)
            amt = 8 - sh if rev else sh
            sh_b = pltpu.roll(acc_b, amt, 0)
            sh_a = pltpu.roll(acc_a, amt, 0)
            acc_b = jnp.where(keep, acc_a * sh_b + acc_b, acc_b)
            acc_a = jnp.where(keep, acc_a * sh_a, acc_a)
        hs = acc_a * carry + acc_b
        if fuse_mix:
            o_ref[0, pl.ds(g0, 8), :] = (gl_ref[0, pl.ds(g0, 8), :] * (hf_ref[0, pl.ds(g0, 8), :] + hs)).astype(BF16)
        else:
            o_ref[0, pl.ds(g0, 8), :] = hs
        return hs[0:1] if rev else hs[7:8]

    carry_ref[...] = lax.fori_loop(0, groups, group, carry_ref[...], unroll=4)


def _lru_tile(n, rev, s_len, r_len, n_tiles):
    tm = ROW_TILE
    n_lat = s_len // tm
    n_ctx = n_tiles - n_lat
    if rev:
        return n_tiles - 1 - n
    return jnp.where(n < n_ctx, n_lat + n, n - n_ctx)


def _lru_scan(xr, conv_w, conv_b, w_gate, b_gate, lam_row, s_len, rev, h_fwd=None, gelu=None):
    b, r, width = xr.shape
    tm = ROW_TILE
    n_tiles = r // tm
    fuse_mix = h_fwd is not None
    tile_of = lambda bi, n: _lru_tile(n, rev, s_len, r, n_tiles)
    full = lambda a: pl.BlockSpec(a.shape, lambda bi, n: (0,) * a.ndim)
    cur = pl.BlockSpec((1, tm, width), lambda bi, n: (bi, tile_of(bi, n), 0))
    extra_args, extra_specs = ([h_fwd, gelu], [cur, cur]) if fuse_mix else ([], [])
    return pl.pallas_call(
        functools.partial(_lru_scan_kernel, rev=rev, fuse_mix=fuse_mix, s_len=s_len, r_len=r, n_tiles=n_tiles),
        grid=(b, n_tiles),
        in_specs=_halo_specs(tm, width, tile_of, r)
        + [full(conv_w), full(conv_b), full(w_gate), full(b_gate), full(lam_row)] + extra_specs,
        out_specs=cur,
        out_shape=jax.ShapeDtypeStruct((b, r, width), BF16 if fuse_mix else F32),
        scratch_shapes=[pltpu.VMEM((tm + 16, width), F32), pltpu.VMEM((tm, width), F32),
                        pltpu.VMEM((tm, width), F32), pltpu.VMEM((1, width), F32)],
        compiler_params=_params("parallel", "arbitrary"),
        name="lru_bwd" if rev else "lru_fwd",
    )(xr, xr, xr, conv_w, conv_b, w_gate, b_gate, lam_row, *extra_args)


def _rope_tables(s_len):
    t = jnp.arange(s_len)
    inv = ROPE_THETA ** (-jnp.arange(ROPE_PAIRS, dtype=F32) / ROPE_PAIRS)
    ang_r = (t // GRID_W).astype(F32)[:, None] * inv
    ang_c = (t % GRID_W).astype(F32)[:, None] * inv
    cos = jnp.concatenate([jnp.cos(ang_r)] * 2 + [jnp.cos(ang_c)] * 2, axis=1)
    sin = jnp.concatenate([-jnp.sin(ang_r), jnp.sin(ang_r), -jnp.sin(ang_c), jnp.sin(ang_c)], axis=1)
    return jnp.tile(cos, (1, 2)), jnp.tile(sin, (1, 2))


def kernel(x, c, ctx, c_ctx, ev_norm1, ev_norm2, ev_ada_w, ev_ada_b, ev_w_in, ev_w_out, ev_q_norm, ev_k_norm, ev_lam_q1, ev_lam_k1, ev_lam_q2, ev_lam_k2, ev_sub_norm, ev_gdn_conv, ev_gdn_a_log, ev_gdn_dt_bias, ev_gdn_o_norm, ev_ffn_w_gu, ev_ffn_w_down, od_norm1, od_norm2, od_ada_w, od_ada_b, od_w_in, od_conv_w, od_conv_b, od_w_r, od_b_r, od_w_i, od_b_i, od_lam, od_w_out, od_ffn_w_gu, od_ffn_w_down):
    batch, s_len, d = x.shape
    assert ev_norm1.shape[0] == 1 and od_norm1.shape[0] == 1, "two-layer block: one even and one odd layer"
    assert ctx.shape[1] % ROW_TILE == 0 and s_len % ROW_TILE == 0 and batch + 1 <= 8
    row = lambda v: v.reshape(1, -1).astype(F32)

    cond = jnp.zeros((8, d), F32).at[:batch].set(c).at[batch].set(c_ctx)

    mod0 = _mod_table(_ada_mod(cond, ev_ada_w[0], ev_ada_b[0]), batch, d)
    w_in = ev_w_in[0].astype(BF16)
    qw = DA_HEADS * 2 * DA_QK
    vw = DA_HEADS * DA_V
    gw = GDN_HEADS * GDN_DK
    o_da, o_gdn = 2 * qw + vw, 2 * qw + vw + 3 * gw
    wq, wk, wv = w_in[:, :qw], w_in[:, qw:2 * qw], w_in[:, 2 * qw:o_da]
    wg, wz = w_in[:, o_da:o_gdn], w_in[:, o_gdn:o_gdn + gw]
    wgt = jnp.pad(w_in[:, o_gdn + gw:], ((0, 0), (0, 128 - 4 * GDN_HEADS)))
    grp = jnp.arange(qw) // DA_QK
    gsum = (grp[:, None] == grp[None, :]).astype(BF16)
    qgain = jnp.tile(ev_q_norm[0], qw // DA_QK).reshape(1, qw)
    kgain = jnp.tile(ev_k_norm[0], qw // DA_QK).reshape(1, qw)
    cos, sin = _rope_tables(s_len)
    qz, k, vt, ug, z, gt = _inproj0(x, ctx, mod0, row(ev_norm1[0]), wq, wk, wv, wg, wz, wgt, gsum, qgain, kgain,
                                    cos, sin)
    lambda_init = 0.8 - 0.6 * math.exp(-0.3 * 0)
    lam_rows = jnp.pad(jnp.stack([ev_lam_q1[0], ev_lam_k1[0], ev_lam_q2[0], ev_lam_k2[0]]),
                       ((0, 0), (0, 128 - DA_QK)))
    ax = _diff_attention(qz, k, vt, ev_sub_norm[0].reshape(DA_V, 1), lam_rows, ev_q_norm[0], ev_k_norm[0], s_len,
                         lambda_init)
    pad16 = lambda v: jnp.zeros((1, 128), F32).at[0, 8:8 + 2 * GDN_HEADS].set(v.reshape(-1))
    prep = _gdn_prep(ug, gt, ev_gdn_conv[0], pad16(ev_gdn_a_log[0]), pad16(ev_gdn_dt_bias[0]), s_len)
    o_f, o_b = _gdn_scan(prep, s_len)
    o_gain = jnp.tile(ev_gdn_o_norm[0], GDN_HEADS).reshape(1, gw)
    h = _post((x, ctx), mod0, (ax, o_f, o_b, z, o_gain), row(ev_norm2[0]), ev_w_out[0].astype(BF16),
              ev_ffn_w_gu[0].astype(BF16), ev_ffn_w_down[0].astype(BF16), s_len, s_len + ctx.shape[1])

    mod1 = _mod_table(_ada_mod(cond, od_ada_w[0], od_ada_b[0]), batch, d)
    w_in1 = od_w_in[0].astype(BF16)
    lw = od_conv_b.shape[1]
    gelu, xr = _inproj1(h, mod1, row(od_norm1[0]), w_in1[:, :lw], w_in1[:, lw:], s_len)
    w_gate = jnp.concatenate([od_w_r[0], od_w_i[0]], axis=-1).astype(BF16)
    b_gate = jnp.stack([od_b_r[0], od_b_i[0]], axis=1)
    h_f = _lru_scan(xr, od_conv_w[0], row(od_conv_b[0]), w_gate[0], b_gate[0], row(od_lam[0, 0]), s_len, False)
    mix = _lru_scan(xr, od_conv_w[0], row(od_conv_b[0]), w_gate[1], b_gate[1], row(od_lam[0, 1]), s_len, True,
                    h_fwd=h_f, gelu=gelu)
    return _post(h, mod1, (mix,), row(od_norm2[0]), od_w_out[0].astype(BF16), od_ffn_w_gu[0].astype(BF16),
                 od_ffn_w_down[0].astype(BF16), s_len, s_len)
```
